```python
import math
import jax, jax.numpy as jnp
from jax import lax
import numpy as np

D_MODEL = 1024
BATCH = 16
SEQ = 4096
DEPTH = 2
DEC_BATCH = 16
DEC_SEQ = 16
PAST_LEN = 2048

CHUNK = 64
Q_BLOCK = 128
EPS = 1e-6
FOX_HEADS = 6
FOX_DIM = 64
FOX_W = FOX_HEADS * FOX_DIM
DSA_HEADS = 4
DSA_DIM = 64
DSA_W = DSA_HEADS * DSA_DIM
IDX_HEADS = 8
IDX_DIM = 32
TOPK_MAX = 256
MLA_HEADS = 6
MLA_Q_LORA = 256
MLA_KV_LORA = 128
MLA_NOPE = 64
MLA_ROPE = 32
MLA_V = 64
MLA_W = MLA_HEADS * MLA_V
ROPE_THETA = 10000.0
REL_BUCKETS = 32
REL_MAX_DIST = 128
D_FF = -(-8 * D_MODEL // (3 * 256)) * 256
IN_SPLITS = (FOX_W, FOX_W, FOX_W, FOX_HEADS,
             DSA_W, DSA_DIM, DSA_DIM, IDX_HEADS * IDX_DIM, IDX_DIM, IDX_HEADS,
             MLA_Q_LORA, MLA_KV_LORA, MLA_ROPE)
D_IN = sum(IN_SPLITS)

kernel_name = 'hybrid_streaming_encoder_step'


def _rms(x, g):
    xf = x.astype(jnp.float32)
    y = xf * lax.rsqrt(jnp.mean(xf * xf, axis=-1, keepdims=True) + EPS) * g.astype(jnp.float32)
    return y.astype(x.dtype)


def _rope(x, pos):
    half = x.shape[-1] // 2
    freq = ROPE_THETA ** (-jnp.arange(half, dtype=jnp.float32) / half)
    ang = pos.astype(jnp.float32)[:, None] * freq[None, :]
    shape = (pos.shape[0],) + (1,) * (x.ndim - 3) + (half,)
    cos = jnp.cos(ang).reshape(shape)
    sin = jnp.sin(ang).reshape(shape)
    x1 = x[..., :half].astype(jnp.float32)
    x2 = x[..., half:].astype(jnp.float32)
    return jnp.concatenate([x1 * cos - x2 * sin, x1 * sin + x2 * cos], axis=-1).astype(x.dtype)


def _t5_bucket(rel):
    nb = REL_BUCKETS // 2
    max_exact = nb // 2
    side = jnp.where(rel > 0, nb, 0)
    n = jnp.abs(rel)
    large = max_exact + (jnp.log(jnp.maximum(n, 1).astype(jnp.float32) / max_exact)
                         / math.log(REL_MAX_DIST / max_exact) * (nb - max_exact)).astype(jnp.int32)
    large = jnp.minimum(large, nb - 1)
    return side + jnp.where(n < max_exact, n, large)


def _split_points(sizes):
    return [int(v) for v in np.cumsum(sizes)[:-1]]


def _over_query_blocks(fn, *args):
    T = args[0].shape[1]
    if T <= Q_BLOCK:
        return fn(*args)
    nb = T // Q_BLOCK
    blocks = tuple(jnp.moveaxis(a.reshape((a.shape[0], nb, Q_BLOCK) + a.shape[2:]), 1, 0) for a in args)
    out = lax.map(lambda blk: fn(*blk), blocks)
    out = jnp.moveaxis(out, 0, 1)
    return out.reshape((out.shape[0], T) + out.shape[3:])


def _layer(x, past, p):
    f32 = jnp.float32
    B, T, _ = x.shape
    P = 0 if past is None else past[0].shape[1]
    S = P + T
    kpos = jnp.arange(S, dtype=jnp.int32)
    qpos = kpos[P:]
    qpos_b = qpos[None, :]
    kchunk = kpos // CHUNK

    def with_past(j, new):
        return new if past is None else jnp.concatenate([past[j].astype(new.dtype), new], axis=1)

    h = _rms(x, p['norm_mix'])
    (fq, fk, fv, fg, bq, bk, bv, iq, ik, iw, cqa, ckva, ckpe) = jnp.split(
        h @ p['w_in'], _split_points(IN_SPLITS), axis=-1)

    fq = _rms(fq.reshape(B, T, FOX_HEADS, FOX_DIM), p['fox_gq'])
    fk = _rms(fk.reshape(B, T, FOX_HEADS, FOX_DIM), p['fox_gk'])
    fv = fv.reshape(B, T, FOX_HEADS, FOX_DIM)
    logf = jax.nn.log_sigmoid(fg.astype(f32) + p['fox_bf'].astype(f32))
    FK, FV, FL = with_past(0, fk), with_past(1, fv), with_past(2, logf)
    cum = jnp.cumsum(FL, axis=1)
    cum_k = jnp.moveaxis(cum, 2, 1)

    def fox_block(q, qp, cq):
        s = jnp.einsum('bqhd,bshd->bhqs', q, FK).astype(f32) * FOX_DIM ** -0.5
        s = s + jnp.moveaxis(cq, 2, 1)[..., None] - cum_k[:, :, None, :]
        s = jnp.where(kpos[None, :] <= qp[0][:, None], s, -jnp.inf)
        a = jax.nn.softmax(s, axis=-1).astype(FV.dtype)
        return jnp.einsum('bhqs,bshd->bqhd', a, FV)

    ya = _over_query_blocks(fox_block, fq, qpos_b, cum[:, P:])

    bq = _rms(bq.reshape(B, T, DSA_HEADS, DSA_DIM), p['dsa_gq'])
    bk = _rms(bk, p['dsa_gk'])
    ik = _rms(ik, p['idx_gk'])
    iq = iq.reshape(B, T, IDX_HEADS, IDX_DIM)
    iw = iw * IDX_HEADS ** -0.5
    BK, BV, IK = with_past(3, bk), with_past(4, bv), with_past(5, ik)
    k_sel = min(TOPK_MAX, S // 4)
    gather = jax.vmap(lambda rows, idx: rows[idx])
    rel_bias = p['rel_bias']

    def dsa_block(q, qp, iq_b, iw_b):
        qp = qp[0]
        qchunk = qp // CHUNK
        sc = jnp.einsum('bqhe,bse->bqhs', iq_b, IK).astype(f32) * IDX_DIM ** -0.5
        score = jnp.einsum('bqhs,bqh->bqs', jax.nn.relu(sc), iw_b.astype(f32))
        score = jnp.where(kchunk[None, None, :] <= qchunk[None, :, None], score, -jnp.inf)
        _, sel = lax.top_k(score, k_sel)
        ks, vs = gather(BK, sel), gather(BV, sel)
        spos = kpos[sel]
        valid = (spos // CHUNK) <= qchunk[None, :, None]
        bias = rel_bias[_t5_bucket(spos - qp[None, :, None])]
        s = (jnp.einsum('bqhd,bqkd->bhqk', q, ks).astype(f32) * DSA_DIM ** -0.5
             + jnp.moveaxis(bias, 3, 1).astype(f32))
        s = jnp.where(valid[:, None], s, -jnp.inf)
        a = jax.nn.softmax(s, axis=-1).astype(vs.dtype)
        return jnp.einsum('bhqk,bqkd->bqhd', a, vs)

    yb = _over_query_blocks(dsa_block, bq, qpos_b, iq, iw)

    qc = (_rms(cqa, p['mla_gqa']) @ p['mla_wqb']).reshape(B, T, MLA_HEADS, MLA_NOPE + MLA_ROPE)
    q_nope = _rms(qc[..., :MLA_NOPE], p['mla_gqn'])
    q_pe = _rope(_rms(qc[..., MLA_NOPE:], p['mla_gqr']), qpos)
    ckv = _rms(ckva, p['mla_gkv'])
    kpe = _rope(_rms(ckpe, p['mla_gkr']), qpos)
    CKV, KPE = with_past(6, ckv), with_past(7, kpe)
    kv = (CKV @ p['mla_wkvb']).reshape(B, S, MLA_HEADS, MLA_NOPE + MLA_V)
    k_nope = _rms(kv[..., :MLA_NOPE], p['mla_gkn'])
    mv = kv[..., MLA_NOPE:]

    def mla_block(qn, qr, qp):
        qp = qp[0]
        s = (jnp.einsum('bqhd,bshd->bhqs', qn, k_nope)
             + jnp.einsum('bqhr,bsr->bhqs', qr, KPE)).astype(f32) * (MLA_NOPE + MLA_ROPE) ** -0.5
        s = jnp.where(kchunk[None, :] <= (qp // CHUNK)[:, None], s, -jnp.inf)
        a = jax.nn.softmax(s, axis=-1).astype(mv.dtype)
        return jnp.einsum('bhqs,bshd->bqhd', a, mv)

    yc = _over_query_blocks(mla_block, q_nope, q_pe, qpos_b)

    ya = ya.reshape(B, T, FOX_W) @ p['w_fox_out']
    yb = yb.reshape(B, T, DSA_W) @ p['w_dsa_out']
    yc = yc.reshape(B, T, MLA_W) @ p['w_mla_out']
    g = jax.nn.sigmoid((h @ p['w_gate']).astype(f32)).astype(x.dtype)
    ga, gb, gc = jnp.split(g, 3, axis=-1)
    x = x + (ga * ya + gb * yb + gc * yc) @ p['w_o']

    h2 = _rms(x, p['norm_ffn'])
    gt, up = jnp.split(h2 @ p['w_ffn_in'], 2, axis=-1)
    x = x + (jax.nn.silu(gt) * up) @ p['w_ffn_out']
    return x, (fk, fv, logf, bk, bv, ik, ckv, kpe)


def setup_inputs(seed: int = 0) -> dict:
    key = jax.random.key(seed)
    ks = jax.random.split(key, 40)

    def nrm(i, shape, scale):
        return scale * jax.random.normal(ks[i], shape, jnp.float32)

    def gain(i, shape):
        return 1.0 + 0.02 * jax.random.normal(ks[i], shape, jnp.float32)

    L = DEPTH
    cb = (DEPTH, DEC_BATCH, PAST_LEN)
    return {
        'x_prompt': nrm(0, (BATCH, SEQ, D_MODEL), 1.0),
        'x_sample': nrm(1, (DEC_BATCH, DEC_SEQ, D_MODEL), 1.0),
        'cache_fox_k': nrm(2, cb + (FOX_HEADS, FOX_DIM), 1.0),
        'cache_fox_v': nrm(3, cb + (FOX_HEADS, FOX_DIM), 1.0),
        'cache_fox_logf': jax.nn.log_sigmoid(1.0 + nrm(4, cb + (FOX_HEADS,), 1.0)),
        'cache_dsa_k': nrm(5, cb + (DSA_DIM,), 1.0),
        'cache_dsa_v': nrm(6, cb + (DSA_DIM,), 1.0),
        'cache_idx_k': nrm(7, cb + (IDX_DIM,), 1.0),
        'cache_mla_ckv': nrm(8, cb + (MLA_KV_LORA,), 1.0),
        'cache_mla_kpe': nrm(9, cb + (MLA_ROPE,), 1.0),
        'rel_bias': nrm(10, (REL_BUCKETS, DSA_HEADS), 0.5),
        'norm_mix': gain(11, (L, D_MODEL)),
        'w_in': nrm(12, (L, D_MODEL, D_IN), D_MODEL ** -0.5),
        'fox_gq': gain(13, (L, FOX_DIM)),
        'fox_gk': gain(14, (L, FOX_DIM)),
        'fox_bf': nrm(15, (L, FOX_HEADS), 0.1),
        'dsa_gq': gain(16, (L, DSA_DIM)),
        'dsa_gk': gain(17, (L, DSA_DIM)),
        'idx_gk': gain(18, (L, IDX_DIM)),
        'mla_gqa': gain(19, (L, MLA_Q_LORA)),
        'mla_wqb': nrm(20, (L, MLA_Q_LORA, MLA_HEADS * (MLA_NOPE + MLA_ROPE)), MLA_Q_LORA ** -0.5),
        'mla_gqn': gain(21, (L, MLA_NOPE)),
        'mla_gqr': gain(22, (L, MLA_ROPE)),
        'mla_gkv': gain(23, (L, MLA_KV_LORA)),
        'mla_gkr': gain(24, (L, MLA_ROPE)),
        'mla_wkvb': nrm(25, (L, MLA_KV_LORA, MLA_HEADS * (MLA_NOPE + MLA_V)), MLA_KV_LORA ** -0.5),
        'mla_gkn': gain(26, (L, MLA_NOPE)),
        'w_fox_out': nrm(27, (L, FOX_W, D_MODEL), FOX_W ** -0.5),
        'w_dsa_out': nrm(28, (L, DSA_W, D_MODEL), DSA_W ** -0.5),
        'w_mla_out': nrm(29, (L, MLA_W, D_MODEL), MLA_W ** -0.5),
        'w_gate': nrm(30, (L, D_MODEL, 3 * D_MODEL), D_MODEL ** -0.5),
        'w_o': nrm(31, (L, D_MODEL, D_MODEL), D_MODEL ** -0.5),
        'norm_ffn': gain(32, (L, D_MODEL)),
        'w_ffn_in': nrm(33, (L, D_MODEL, 2 * D_FF), D_MODEL ** -0.5),
        'w_ffn_out': nrm(34, (L, D_FF, D_MODEL), D_FF ** -0.5),
    }


def reference(x_prompt, x_sample, cache_fox_k, cache_fox_v, cache_fox_logf, cache_dsa_k, cache_dsa_v,
              cache_idx_k, cache_mla_ckv, cache_mla_kpe, rel_bias, norm_mix, w_in, fox_gq, fox_gk, fox_bf,
              dsa_gq, dsa_gk, idx_gk, mla_gqa, mla_wqb, mla_gqn, mla_gqr, mla_gkv, mla_gkr, mla_wkvb,
              mla_gkn, w_fox_out, w_dsa_out, w_mla_out, w_gate, w_o, norm_ffn, w_ffn_in, w_ffn_out):
    caches = (cache_fox_k, cache_fox_v, cache_fox_logf, cache_dsa_k, cache_dsa_v,
              cache_idx_k, cache_mla_ckv, cache_mla_kpe)
    yp, ys = x_prompt, x_sample
    p_rows, s_rows = [], []
    for i in range(DEPTH):
        p = dict(rel_bias=rel_bias, norm_mix=norm_mix[i], w_in=w_in[i], fox_gq=fox_gq[i], fox_gk=fox_gk[i],
                 fox_bf=fox_bf[i], dsa_gq=dsa_gq[i], dsa_gk=dsa_gk[i], idx_gk=idx_gk[i],
                 mla_gqa=mla_gqa[i], mla_wqb=mla_wqb[i], mla_gqn=mla_gqn[i], mla_gqr=mla_gqr[i],
                 mla_gkv=mla_gkv[i], mla_gkr=mla_gkr[i], mla_wkvb=mla_wkvb[i], mla_gkn=mla_gkn[i],
                 w_fox_out=w_fox_out[i], w_dsa_out=w_dsa_out[i], w_mla_out=w_mla_out[i],
                 w_gate=w_gate[i], w_o=w_o[i], norm_ffn=norm_ffn[i], w_ffn_in=w_ffn_in[i],
                 w_ffn_out=w_ffn_out[i])
        yp, rows_p = _layer(yp, None, p)
        ys, rows_s = _layer(ys, tuple(c[i] for c in caches), p)
        p_rows.append(rows_p)
        s_rows.append(rows_s)

    def st(rows, j):
        return jnp.stack([r[j] for r in rows], axis=0)

    return (yp, ys,
            st(p_rows, 0), st(p_rows, 1), st(p_rows, 2), st(p_rows, 3),
            st(p_rows, 4), st(p_rows, 5), st(p_rows, 6), st(p_rows, 7),
            st(s_rows, 0), st(s_rows, 1), st(s_rows, 2), st(s_rows, 3),
            st(s_rows, 4), st(s_rows, 5), st(s_rows, 6), st(s_rows, 7))
```

```python
import functools
import math

import numpy as np
import jax
import jax.numpy as jnp
from jax import lax
from jax.experimental import pallas as pl
from jax.experimental.pallas import tpu as pltpu

F32 = jnp.float32
BF16 = jnp.bfloat16

CHUNK = 64
EPS = 1e-6
FOX_HEADS, FOX_DIM = 6, 64
FOX_W = FOX_HEADS * FOX_DIM
DSA_HEADS, DSA_DIM = 4, 64
DSA_W = DSA_HEADS * DSA_DIM
IDX_HEADS, IDX_DIM = 8, 32
IDX_W = IDX_HEADS * IDX_DIM
TOPK_MAX = 256
MLA_HEADS = 6
MLA_Q_LORA, MLA_KV_LORA = 256, 128
MLA_NOPE, MLA_ROPE, MLA_V = 64, 32, 64
MLA_W = MLA_HEADS * MLA_V
ROPE_THETA = 10000.0
REL_BUCKETS, REL_MAX_DIST = 32, 128

LANE = 128
KEY_BLOCK = 128
M_INIT = -1e30
MASKED = -3e30
INT_MIN = -2 ** 31
KEY_NEG_INF = -2139095041
VMEM_LIMIT = 56 * 1024 * 1024

_C_FQ, _C_FK, _C_FV, _C_FG = 0, 384, 768, 1152
_C_BQ, _C_BK, _C_BV = 1280, 1536, 1792
_C_IQ, _C_IK, _C_IW = 1920, 2176, 2432
_C_QA, _C_KV, _C_PE = 2560, 2816, 2944
_C_END = 3072


def _dot(a, b):
    return jnp.dot(a, b, preferred_element_type=F32)


def _dot_nt(a, b):
    return lax.dot_general(a, b, (((1,), (1,)), ((), ())), preferred_element_type=F32)


def _dot_hilo(x, m):
    hi = x.astype(BF16)
    lo = (x - hi.astype(F32)).astype(BF16)
    return _dot(hi, m) + _dot(lo, m)


def _rms_rows(x, g):
    return x * lax.rsqrt(jnp.mean(x * x, axis=-1, keepdims=True) + EPS) * g


def _rms_groups(x, avg, g):
    return x * lax.rsqrt(_dot_hilo(x * x, avg) + EPS) * g


def _params(sem, vmem=VMEM_LIMIT):
    return pltpu.CompilerParams(dimension_semantics=sem, vmem_limit_bytes=vmem)


def _const_spec(shape):
    nd = len(shape)
    return pl.BlockSpec(shape, lambda *_: (0,) * nd)


def _proj_kernel(x_ref, gmix_ref, w_ref, bf_ref, gfq_ref, gfk_ref, gbq_ref, gbk_ref, gik_ref,
                 gqa_ref, gkv_ref, gkr_ref, wqb_ref, gqc_ref,
                 a64x6_ref, a64x4_ref, a32x8_ref, aqc_ref, ape_ref, rq_ref, rk_ref,
                 cosq_ref, sinq_ref, cosk_ref, sink_ref,
                 fq_o, fkf_o, fkb_o, fvf_o, fvb_o, lf_o, bq_o, bkf_o, bkr_o, bvf_o, bvb_o,
                 iq_o, ikf_o, ikr_o, iw_o, qc_o, ckv_o, kpe_o):
    x = x_ref[...]
    hb = _rms_rows(x, gmix_ref[...]).astype(BF16)

    def grp(lo, hi):
        return _dot(hb, w_ref[:, lo:hi])

    fq_o[...] = _rms_groups(grp(_C_FQ, _C_FK), a64x6_ref[...], gfq_ref[...]).astype(BF16)
    fk = _rms_groups(grp(_C_FK, _C_FV), a64x6_ref[...], gfk_ref[...])
    fkf_o[...] = fk
    fkb_o[...] = fk.astype(BF16)
    fv = grp(_C_FV, _C_FG)
    fvf_o[...] = fv
    fvb_o[...] = fv.astype(BF16)
    z = grp(_C_FG, _C_BQ) + bf_ref[...]
    lf = jnp.minimum(z, 0.0) - jnp.log1p(jnp.exp(-jnp.abs(z)))
    lf_o[...] = lf[:, :8]

    bq_o[...] = _rms_groups(grp(_C_BQ, _C_BK), a64x4_ref[...], gbq_ref[...]).astype(BF16)
    bk = _rms_groups(grp(_C_BK, _C_BV), a64x4_ref[...], gbk_ref[...])
    bkf_o[...] = bk[:, :DSA_DIM]
    bkr_o[...] = bk.astype(BF16)
    bv = grp(_C_BV, _C_IQ)[:, :DSA_DIM]
    bvf_o[...] = bv
    bvb_o[...] = bv.astype(BF16)
    iq_o[...] = grp(_C_IQ, _C_IK).astype(BF16)
    ik = _rms_groups(grp(_C_IK, _C_IW), a32x8_ref[...], gik_ref[...])
    ikf_o[...] = ik[:, :IDX_DIM]
    ikr_o[...] = ik.astype(BF16)
    iw_o[...] = (grp(_C_IW, _C_QA) * (1.0 / 16.0))[:, :8]

    cq = _rms_rows(grp(_C_QA, _C_KV), gqa_ref[...]).astype(BF16)
    cosq, sinq = cosq_ref[...], sinq_ref[...]
    for h in range(MLA_HEADS):
        sl = slice(h * LANE, (h + 1) * LANE)
        qh = _rms_groups(_dot(cq, wqb_ref[:, sl]), aqc_ref[...], gqc_ref[:, sl])
        qc_o[:, sl] = (qh * cosq + _dot_hilo(qh, rq_ref[...]) * sinq).astype(BF16)
    ckv_o[...] = _rms_rows(grp(_C_KV, _C_PE), gkv_ref[...])
    kp = _rms_groups(grp(_C_PE, _C_END), ape_ref[...], gkr_ref[...])
    kp = kp * cosk_ref[...] + _dot_hilo(kp, rk_ref[...]) * sink_ref[...]
    kpe_o[...] = kp[:, :MLA_ROPE]


def _proj_call(x2d, consts, tabs, tm, n_tab_blocks):
    n = x2d.shape[0]
    row = lambda w: pl.BlockSpec((tm, w), lambda i: (i, 0))
    tab = pl.BlockSpec((tm, LANE), lambda i: (i % n_tab_blocks, 0))
    in_specs = [row(x2d.shape[1])] + [_const_spec(c.shape) for c in consts] + [tab] * 4
    widths = [(FOX_W, BF16), (FOX_W, F32), (FOX_W, BF16), (FOX_W, F32), (FOX_W, BF16), (8, F32),
              (DSA_W, BF16), (DSA_DIM, F32), (DSA_W, BF16), (DSA_DIM, F32), (DSA_DIM, BF16),
              (IDX_W, BF16), (IDX_DIM, F32), (IDX_W, BF16), (8, F32),
              (MLA_HEADS * LANE, BF16), (MLA_KV_LORA, F32), (MLA_ROPE, F32)]
    return pl.pallas_call(
        _proj_kernel,
        grid=(n // tm,),
        in_specs=in_specs,
        out_specs=[row(w) for w, _ in widths],
        out_shape=[jax.ShapeDtypeStruct((n, w), d) for w, d in widths],
        compiler_params=_params(("parallel",)),
        name="proj",
    )(x2d, *consts, *tabs)


def _cumsum_kernel(x_ref, tri_ref, o_ref):
    nblk = x_ref.shape[2] // LANE

    def body(j, carry):
        s = pl.multiple_of(j * LANE, LANE)
        x = x_ref[0, :, pl.ds(s, LANE)]
        x1 = x.astype(BF16)
        r1 = x - x1.astype(F32)
        x2 = r1.astype(BF16)
        x3 = (r1 - x2.astype(F32)).astype(BF16)
        tri = tri_ref[...]
        out = _dot(x1, tri) + _dot(x2, tri) + _dot(x3, tri) + carry
        o_ref[0, :, pl.ds(s, LANE)] = out
        return out[:, LANE - 1:LANE]

    lax.fori_loop(0, nblk, body, jnp.zeros((8, 1), F32))


def _cumsum_call(lf_t):
    b, _, s = lf_t.shape
    tri = jnp.asarray(np.triu(np.ones((LANE, LANE), np.float32)), BF16)
    return pl.pallas_call(
        _cumsum_kernel,
        grid=(b,),
        in_specs=[pl.BlockSpec((1, 8, s), lambda i: (i, 0, 0)), _const_spec((LANE, LANE))],
        out_specs=pl.BlockSpec((1, 8, s), lambda i: (i, 0, 0)),
        out_shape=jax.ShapeDtypeStruct((b, 8, s), F32),
        compiler_params=_params(("parallel",)),
        name="cumsum",
    )(lf_t, tri)


def _fox_kernel(q_ref, k_ref, v_ref, cq_ref, ck_ref, o_ref, *, tq, tk, past):
    q0 = past + pl.program_id(2) * tq
    nkb = (q0 + tq + tk - 1) // tk
    q = q_ref[0]
    lane = lax.broadcasted_iota(jnp.int32, (tq, LANE), 1)
    qpos = q0 + lax.broadcasted_iota(jnp.int32, (tq, tk), 0)
    koff = lax.broadcasted_iota(jnp.int32, (tq, tk), 1)
    outs = []
    for half in range(2):
        qh = jnp.where((lane >= 64 * half) & (lane < 64 * (half + 1)), q, jnp.zeros_like(q))
        cq = cq_ref[0, 0][:, half:half + 1]

        def body(j, carry, qh=qh, cq=cq, half=half):
            m, l, acc = carry
            ks = pl.multiple_of(j * tk, tk)
            kb = k_ref[0, pl.ds(ks, tk), :]
            vb = v_ref[0, pl.ds(ks, tk), :]
            ck = ck_ref[0, 0, half:half + 1, pl.ds(ks, tk)]
            s = _dot_nt(qh, kb) + cq - ck
            s = jnp.where(ks + koff <= qpos, s, MASKED)
            m_new = jnp.maximum(m, jnp.max(s, axis=1, keepdims=True))
            alpha = jnp.exp(m - m_new)
            p = jnp.exp(s - m_new)
            l = alpha * l + jnp.sum(p, axis=1, keepdims=True)
            acc = alpha * acc + _dot(p.astype(BF16), vb)
            return m_new, l, acc

        init = (jnp.full((tq, 1), M_INIT, F32), jnp.zeros((tq, 1), F32), jnp.zeros((tq, LANE), F32))
        _, l, acc = lax.fori_loop(0, nkb, body, init)
        outs.append(acc / l)
    o_ref[0] = jnp.where(lane < 64, outs[0], outs[1]).astype(o_ref.dtype)


def _fox_call(q, k, v, cq, ck, tq, tk, past):
    b, t, _ = q.shape
    s = k.shape[1]
    return pl.pallas_call(
        functools.partial(_fox_kernel, tq=tq, tk=tk, past=past),
        grid=(b, FOX_HEADS // 2, t // tq),
        in_specs=[
            pl.BlockSpec((1, tq, LANE), lambda i, p, j: (i, j, p)),
            pl.BlockSpec((1, s, LANE), lambda i, p, j: (i, 0, p)),
            pl.BlockSpec((1, s, LANE), lambda i, p, j: (i, 0, p)),
            pl.BlockSpec((1, 1, tq, 2), lambda i, p, j: (i, p, j, 0)),
            pl.BlockSpec((1, 1, 2, s), lambda i, p, j: (i, p, 0, 0)),
        ],
        out_specs=pl.BlockSpec((1, tq, LANE), lambda i, p, j: (i, j, p)),
        out_shape=jax.ShapeDtypeStruct((b, t, FOX_W), BF16),
        compiler_params=_params(("parallel", "parallel", "parallel")),
        name="fox",
    )(q, k, v, cq, ck)


def _mla_kernel(q_ref, k_ref, v_ref, o_ref, *, tq, tk, past, s_valid):
    q0 = past + pl.program_id(2) * tq
    k_end = jnp.minimum(((q0 + tq - 1) // CHUNK + 1) * CHUNK, s_valid)
    nkb = (k_end + tk - 1) // tk
    lane = lax.broadcasted_iota(jnp.int32, (tq, LANE), 1)
    qchunk = (q0 + lax.broadcasted_iota(jnp.int32, (tq, tk), 0)) // CHUNK
    koff = lax.broadcasted_iota(jnp.int32, (tq, tk), 1)
    outs = []
    for half in range(2):
        sl = slice(half * LANE, (half + 1) * LANE)
        qh = q_ref[0, :, sl]

        def body(j, carry, qh=qh, sl=sl):
            m, l, acc = carry
            ks = pl.multiple_of(j * tk, tk)
            kb = k_ref[0, pl.ds(ks, tk), sl]
            vb = v_ref[0, pl.ds(ks, tk), :]
            kpos = ks + koff
            s = _dot_nt(qh, kb)
            s = jnp.where((kpos // CHUNK <= qchunk) & (kpos < s_valid), s, MASKED)
            m_new = jnp.maximum(m, jnp.max(s, axis=1, keepdims=True))
            alpha = jnp.exp(m - m_new)
            p = jnp.exp(s - m_new)
            l = alpha * l + jnp.sum(p, axis=1, keepdims=True)
            acc = alpha * acc + _dot(p.astype(BF16), vb)
            return m_new, l, acc

        init = (jnp.full((tq, 1), M_INIT, F32), jnp.zeros((tq, 1), F32), jnp.zeros((tq, LANE), F32))
        _, l, acc = lax.fori_loop(0, nkb, body, init)
        outs.append(acc / l)
    o_ref[0] = jnp.where(lane < 64, outs[0], outs[1]).astype(o_ref.dtype)


def _mla_call(q, k, v, tq, tk, past, s_valid):
    b, t, _ = q.shape
    s = k.shape[1]
    return pl.pallas_call(
        functools.partial(_mla_kernel, tq=tq, tk=tk, past=past, s_valid=s_valid),
        grid=(b, MLA_HEADS // 2, t // tq),
        in_specs=[
            pl.BlockSpec((1, tq, 2 * LANE), lambda i, p, j: (i, j, p)),
            pl.BlockSpec((1, s, 2 * LANE), lambda i, p, j: (i, 0, p)),
            pl.BlockSpec((1, s, LANE), lambda i, p, j: (i, 0, p)),
        ],
        out_specs=pl.BlockSpec((1, tq, LANE), lambda i, p, j: (i, j, p)),
        out_shape=jax.ShapeDtypeStruct((b, t, MLA_W), BF16),
        compiler_params=_params(("parallel", "parallel", "parallel")),
        name="mla",
    )(q, k, v)


def _kvup_kernel(ckv_ref, kpe_ref, wk_ref, wv_ref, gkn_ref, akn_ref, place_ref, k_o, v_o):
    cb = ckv_ref[...].astype(BF16)
    v_o[...] = _dot(cb, wv_ref[...]).astype(BF16)
    pe = _dot(kpe_ref[...], place_ref[...])
    for h in range(MLA_HEADS):
        sl = slice(h * LANE, (h + 1) * LANE)
        kn = _rms_groups(_dot(cb, wk_ref[:, sl]), akn_ref[...], gkn_ref[...])
        k_o[:, sl] = (kn + pe).astype(BF16)


def _kvup_call(ckv2d, kpe2d, consts, tm):
    n = ckv2d.shape[0]
    row = lambda w: pl.BlockSpec((tm, w), lambda i: (i, 0))
    return pl.pallas_call(
        _kvup_kernel,
        grid=(n // tm,),
        in_specs=[row(MLA_KV_LORA), row(LANE)] + [_const_spec(c.shape) for c in consts],
        out_specs=[row(MLA_HEADS * LANE), row(MLA_W)],
        out_shape=[jax.ShapeDtypeStruct((n, MLA_HEADS * LANE), BF16),
                   jax.ShapeDtypeStruct((n, MLA_W), BF16)],
        compiler_params=_params(("parallel",)),
        name="kvup",
    )(ckv2d, kpe2d, *consts)


def _dsa_kernel(far_ref, iq_ref, iw_ref, ik_ref, bq_ref, bk_ref, bvt_ref, bias_ref, low_ref,
                o_ref, keys_ref, *, tq, past, s_valid, k_sel):
    kb = KEY_BLOCK
    q0 = past + pl.program_id(1) * tq
    nkb = (jnp.minimum(q0 + tq, s_valid) + kb - 1) // kb
    n_far = jnp.maximum(q0 // kb - 1, 0)
    qpos = q0 + lax.broadcasted_iota(jnp.int32, (kb, tq), 1)
    koff = lax.broadcasted_iota(jnp.int32, (kb, tq), 0)
    qlane = lax.broadcasted_iota(jnp.int32, (tq, IDX_W), 1)

    iq = iq_ref[0]
    iq_heads = [jnp.where(qlane // IDX_DIM == h, iq, jnp.zeros_like(iq)) for h in range(IDX_HEADS)]

    def score_body(j, _):
        ks = pl.multiple_of(j * kb, kb)
        ikb = ik_ref[0, pl.ds(ks, kb), :]
        acc = jnp.zeros((kb, tq), F32)
        for h in range(IDX_HEADS):
            acc = acc + jnp.maximum(_dot_nt(ikb, iq_heads[h]), 0.0) * iw_ref[0, h:h + 1, :]
        kpos = ks + koff
        adm = (kpos // CHUNK <= qpos // CHUNK) & (kpos < s_valid)
        bits = pltpu.bitcast(acc, jnp.int32)
        key = jnp.where(bits < 0, bits ^ 0x7FFFFFFF, bits)
        keys_ref[pl.ds(ks, kb), :] = jnp.where(adm, key, KEY_NEG_INF)
        return 0

    lax.fori_loop(0, nkb, score_body, 0)

    def count(pred_of_key):
        def body(j, c8):
            ks = pl.multiple_of(j * kb, kb)
            hit = jnp.where(pred_of_key(keys_ref[pl.ds(ks, kb), :]), 1, 0)
            return c8 + jnp.sum(hit.reshape(kb // 8, 8, tq), axis=0)
        c8 = lax.fori_loop(0, nkb, body, jnp.zeros((8, tq), jnp.int32))
        return jnp.sum(c8, axis=0, keepdims=True)

    def bit_body(i, ub):
        c = ub | jnp.left_shift(jnp.int32(1), 31 - i)
        cand = c ^ INT_MIN
        return jnp.where(count(lambda key: key >= cand) >= k_sel, c, ub)

    thr = lax.fori_loop(0, 32, bit_body, jnp.zeros((1, tq), jnp.int32)) ^ INT_MIN
    need = (k_sel - count(lambda key: key > thr)).astype(F32)
    thr_next = jnp.maximum(thr, KEY_NEG_INF) + 1

    bq = bq_ref[0]
    blane = lax.broadcasted_iota(jnp.int32, (tq, DSA_W), 1)
    bq_heads = [jnp.where(blane // DSA_DIM == h, bq, jnp.zeros_like(bq)) for h in range(DSA_HEADS)]

    def att_body(j, carry, near):
        seen, state = carry
        ks = pl.multiple_of(j * kb, kb)
        key = keys_ref[pl.ds(ks, kb), :]
        tie = jnp.where(key == thr, 1.0, 0.0)
        rank = seen + _dot(low_ref[...], tie.astype(BF16))
        sel = key >= jnp.maximum(jnp.where(rank < need, thr, thr_next), KEY_NEG_INF + 1)
        bkb = bk_ref[0, pl.ds(ks, kb), :]
        vt = bvt_ref[0, :, pl.ds(ks, kb)]
        if near:
            bs = pl.multiple_of((j - (q0 // kb - 1)) * kb, kb)
        new_state = []
        for h in range(DSA_HEADS):
            m, l, acc = state[h]
            s = _dot_nt(bkb, bq_heads[h])
            if near:
                s = s + bias_ref[h, pl.ds(bs, kb), :]
            else:
                s = s + far_ref[h]
            s = jnp.where(sel, s, MASKED)
            m_new = jnp.maximum(m, jnp.max(s, axis=0, keepdims=True))
            alpha = jnp.exp(m - m_new)
            p = jnp.exp(s - m_new)
            l = alpha * l + jnp.sum(p, axis=0, keepdims=True)
            acc = alpha * acc + _dot(vt, p.astype(BF16))
            new_state.append((m_new, l, acc))
        return seen + jnp.sum(tie, axis=0, keepdims=True), tuple(new_state)

    state0 = tuple((jnp.full((1, tq), M_INIT, F32), jnp.zeros((1, tq), F32),
                    jnp.zeros((DSA_DIM, tq), F32)) for _ in range(DSA_HEADS))
    carry = (jnp.zeros((1, tq), F32), state0)
    carry = lax.fori_loop(0, n_far, functools.partial(att_body, near=False), carry)
    carry = lax.fori_loop(n_far, nkb, functools.partial(att_body, near=True), carry)
    out_t = jnp.concatenate([acc / l for _, l, acc in carry[1]], axis=0)
    o_ref[0] = out_t.T.astype(o_ref.dtype)


def _dsa_call(far, iq, iw_t, ik, bq, bk, bv_t, bias_t, tq, past, s_valid, k_sel):
    b, t, _ = iq.shape
    s = ik.shape[1]
    low = jnp.asarray(np.tril(np.ones((KEY_BLOCK, KEY_BLOCK), np.float32), -1), BF16)
    return pl.pallas_call(
        functools.partial(_dsa_kernel, tq=tq, past=past, s_valid=s_valid, k_sel=k_sel),
        grid=(b, t // tq),
        in_specs=[
            pl.BlockSpec(memory_space=pltpu.SMEM),
            pl.BlockSpec((1, tq, IDX_W), lambda i, j: (i, j, 0)),
            pl.BlockSpec((1, 8, tq), lambda i, j: (i, 0, j)),
            pl.BlockSpec((1, s, IDX_W), lambda i, j: (i, 0, 0)),
            pl.BlockSpec((1, tq, DSA_W), lambda i, j: (i, j, 0)),
            pl.BlockSpec((1, s, DSA_W), lambda i, j: (i, 0, 0)),
            pl.BlockSpec((1, DSA_DIM, s), lambda i, j: (i, 0, 0)),
            _const_spec(bias_t.shape),
            _const_spec(low.shape),
        ],
        out_specs=pl.BlockSpec((1, tq, DSA_W), lambda i, j: (i, j, 0)),
        out_shape=jax.ShapeDtypeStruct((b, t, DSA_W), BF16),
        scratch_shapes=[pltpu.VMEM((s, tq), jnp.int32)],
        compiler_params=_params(("parallel", "parallel")),
        name="dsa",
    )(far, iq, iw_t, ik, bq, bk, bv_t, bias_t, low)


def _merge_kernel(x_ref, ya_ref, yb_ref, yc_ref, g_ref, wg_ref, wa_ref, wb_ref, wc_ref, wo_ref, o_ref):
    x = x_ref[...]
    d = x.shape[1]
    hb = _rms_rows(x, g_ref[...]).astype(BF16)
    mix = jnp.zeros_like(x)
    for i, (y_ref, w_ref) in enumerate(((ya_ref, wa_ref), (yb_ref, wb_ref), (yc_ref, wc_ref))):
        gate = jax.nn.sigmoid(_dot(hb, wg_ref[:, i * d:(i + 1) * d]))
        mix = mix + gate * _dot(y_ref[...], w_ref[...])
    o_ref[...] = x + _dot(mix.astype(BF16), wo_ref[...])


def _merge_call(x2d, ya, yb, yc, consts, tm):
    n, d = x2d.shape
    row = lambda w: pl.BlockSpec((tm, w), lambda i: (i, 0))
    return pl.pallas_call(
        _merge_kernel,
        grid=(n // tm,),
        in_specs=[row(d), row(FOX_W), row(DSA_W), row(MLA_W)] + [_const_spec(c.shape) for c in consts],
        out_specs=row(d),
        out_shape=jax.ShapeDtypeStruct((n, d), F32),
        compiler_params=_params(("parallel",)),
        name="merge",
    )(x2d, ya, yb, yc, *consts)


def _ffn_kernel(x_ref, g_ref, wi_ref, wo_ref, o_ref, *, n_chunks):
    x = x_ref[...]
    hb = _rms_rows(x, g_ref[...]).astype(BF16)
    out = x
    for c in range(n_chunks):
        gu = _dot(hb, wi_ref[c])
        half = gu.shape[1] // 2
        gt, up = gu[:, :half], gu[:, half:]
        out = out + _dot((gt * jax.nn.sigmoid(gt) * up).astype(BF16), wo_ref[c])
    o_ref[...] = out


def _ffn_call(x2d, g, wi, wo, tm):
    n, d = x2d.shape
    row = pl.BlockSpec((tm, d), lambda i: (i, 0))
    return pl.pallas_call(
        functools.partial(_ffn_kernel, n_chunks=wi.shape[0]),
        grid=(n // tm,),
        in_specs=[row, _const_spec(g.shape), _const_spec(wi.shape), _const_spec(wo.shape)],
        out_specs=row,
        out_shape=jax.ShapeDtypeStruct((n, d), F32),
        compiler_params=_params(("parallel",)),
        name="ffn",
    )(x2d, g, wi, wo)


def _block_avg(blocks, width):
    m = np.zeros((width, width), np.float32)
    for lo, hi in blocks:
        m[lo:hi, lo:hi] = 1.0 / (hi - lo)
    return jnp.asarray(m, BF16)


def _rot_matrix(base):
    r = np.zeros((LANE, LANE), np.float32)
    half = MLA_ROPE // 2
    for i in range(half):
        r[base + half + i, base + i] = -1.0
        r[base + i, base + half + i] = 1.0
    return jnp.asarray(r, BF16)


def _rope_tables(pos, base):
    half = MLA_ROPE // 2
    freq = ROPE_THETA ** (-jnp.arange(half, dtype=F32) / half)
    ang = pos.astype(F32)[:, None] * freq[None, :]
    cos, sin = jnp.cos(ang), jnp.sin(ang)
    n = pos.shape[0]
    ct = jnp.ones((n, LANE), F32).at[:, base:base + MLA_ROPE].set(jnp.concatenate([cos, cos], 1))
    st = jnp.zeros((n, LANE), F32).at[:, base:base + MLA_ROPE].set(jnp.concatenate([sin, sin], 1))
    return ct, st


def _t5_bucket(rel):
    nb = REL_BUCKETS // 2
    max_exact = nb // 2
    side = jnp.where(rel > 0, nb, 0)
    n = jnp.abs(rel)
    large = max_exact + (jnp.log(jnp.maximum(n, 1).astype(F32) / max_exact)
                         / math.log(REL_MAX_DIST / max_exact) * (nb - max_exact)).astype(jnp.int32)
    large = jnp.minimum(large, nb - 1)
    return side + jnp.where(n < max_exact, n, large)


def _bias_tables(rel_bias, tq):
    c = jnp.arange(tq + KEY_BLOCK, dtype=jnp.int32)[:, None]
    r = jnp.arange(tq, dtype=jnp.int32)[None, :]
    near = jnp.moveaxis(rel_bias[_t5_bucket(c - KEY_BLOCK - r)], 2, 0).astype(F32)
    far = rel_bias[_t5_bucket(jnp.int32(-REL_MAX_DIST))].astype(F32)
    return near, far


def _pad_cols(w, width):
    return jnp.pad(w, ((0, 0), (0, width - w.shape[1])))


def _tile_vec(g, reps, scale=1.0):
    return (jnp.tile(g.astype(F32), reps) * scale)[None, :]


def _layer_consts(p):
    splits = np.cumsum([FOX_W, FOX_W, FOX_W, FOX_HEADS, DSA_W, DSA_DIM, DSA_DIM, IDX_W, IDX_DIM, IDX_HEADS,
                        MLA_Q_LORA, MLA_KV_LORA, MLA_ROPE])[:-1]
    fq, fk, fv, fg, bq, bk, bv, iq, ik, iw, cqa, ckva, ckpe = jnp.split(p['w_in'], [int(v) for v in splits], axis=1)
    w_all = jnp.concatenate([
        fq, fk, fv, _pad_cols(fg, LANE), bq, jnp.tile(bk, (1, DSA_HEADS)), _pad_cols(bv, LANE),
        iq, jnp.tile(ik, (1, IDX_HEADS)), _pad_cols(iw, LANE), cqa, ckva, _pad_cols(ckpe, LANE)],
        axis=1).astype(BF16)
    assert w_all.shape[1] == _C_END
    d_qk = MLA_NOPE + MLA_ROPE
    wqb = p['mla_wqb'].reshape(MLA_Q_LORA, MLA_HEADS, d_qk)
    wqb = jnp.pad(wqb, ((0, 0), (0, 0), (0, LANE - d_qk))).reshape(MLA_Q_LORA, MLA_HEADS * LANE).astype(BF16)
    gqc = jnp.concatenate([p['mla_gqn'], p['mla_gqr'], jnp.zeros((LANE - d_qk,), F32)]).astype(F32)
    proj = [
        p['norm_mix'].astype(F32)[None, :], w_all,
        _pad_cols(p['fox_bf'].astype(F32)[None, :], LANE),
        _tile_vec(p['fox_gq'], FOX_HEADS, FOX_DIM ** -0.5), _tile_vec(p['fox_gk'], FOX_HEADS),
        _tile_vec(p['dsa_gq'], DSA_HEADS, DSA_DIM ** -0.5), _tile_vec(p['dsa_gk'], DSA_HEADS),
        _tile_vec(p['idx_gk'], IDX_HEADS),
        p['mla_gqa'].astype(F32)[None, :], p['mla_gkv'].astype(F32)[None, :],
        _pad_cols(p['mla_gkr'].astype(F32)[None, :], LANE),
        wqb, _tile_vec(gqc, MLA_HEADS, d_qk ** -0.5),
        _block_avg([(i * 64, i * 64 + 64) for i in range(6)], FOX_W),
        _block_avg([(i * 64, i * 64 + 64) for i in range(4)], DSA_W),
        _block_avg([(i * 32, i * 32 + 32) for i in range(8)], IDX_W),
        _block_avg([(0, MLA_NOPE), (MLA_NOPE, d_qk)], LANE),
        _block_avg([(0, MLA_ROPE)], LANE),
        _rot_matrix(MLA_NOPE), _rot_matrix(0),
    ]
    wkvb = p['mla_wkvb'].reshape(MLA_KV_LORA, MLA_HEADS, MLA_NOPE + MLA_V)
    wk = jnp.pad(wkvb[:, :, :MLA_NOPE], ((0, 0), (0, 0), (0, LANE - MLA_NOPE)))
    wk = wk.reshape(MLA_KV_LORA, MLA_HEADS * LANE).astype(BF16)
    wv = wkvb[:, :, MLA_NOPE:].reshape(MLA_KV_LORA, MLA_W).astype(BF16)
    place = np.zeros((LANE, LANE), np.float32)
    place[np.arange(MLA_ROPE), MLA_NOPE + np.arange(MLA_ROPE)] = 1.0
    kvup = [wk, wv, _pad_cols(p['mla_gkn'].astype(F32)[None, :], LANE),
            _block_avg([(0, MLA_NOPE)], LANE), jnp.asarray(place, BF16)]
    merge = [p['norm_mix'].astype(F32)[None, :], p['w_gate'].astype(BF16), p['w_fox_out'].astype(BF16),
             p['w_dsa_out'].astype(BF16), p['w_mla_out'].astype(BF16), p['w_o'].astype(BF16)]
    d_ff = p['w_ffn_out'].shape[0]
    n_chunks = 2
    ck = d_ff // n_chunks
    wi = p['w_ffn_in']
    wi = jnp.stack([jnp.concatenate([wi[:, c * ck:(c + 1) * ck], wi[:, d_ff + c * ck:d_ff + (c + 1) * ck]], 1)
                    for c in range(n_chunks)]).astype(BF16)
    wo = p['w_ffn_out'].reshape(n_chunks, ck, -1).astype(BF16)
    ffn = [p['norm_ffn'].astype(F32)[None, :], wi, wo]
    return proj, kvup, merge, ffn


def _round_up(v, m):
    return -(-v // m) * m


def _layer(x, past, p, rel_bias):
    b, t, d = x.shape
    pl_len = 0 if past is None else past[0].shape[1]
    s_valid = pl_len + t
    s_pad = _round_up(s_valid, 256)
    n = b * t
    tm = min(512, n)
    tq = min(256, t)
    tk = 256
    tq_dsa = 256 if t >= 256 else KEY_BLOCK
    t_dsa = _round_up(t, tq_dsa)
    k_sel = min(TOPK_MAX, s_valid // 4)
    proj_c, kvup_c, merge_c, ffn_c = _layer_consts(p)

    pos = pl_len + jnp.arange(t, dtype=jnp.int32)
    tabs = _rope_tables(pos, MLA_NOPE) + _rope_tables(pos, 0)
    if t < tm:
        tabs = tuple(jnp.tile(a, (tm // t, 1)) for a in tabs)
    n_tab_blocks = max(t // tm, 1)

    x2d = x.reshape(n, d)
    (fq, fk_f, fk_b, fv_f, fv_b, lf8, bq, bk_f, bk_r, bv_f, bv_b, iq, ik_f, ik_r, iw8, qc, ckv_f, kpe_f
     ) = _proj_call(x2d, proj_c, tabs, tm, n_tab_blocks)

    def seq(a):
        return a.reshape(b, t, a.shape[-1])

    def with_past(cached, new, dtype):
        new = seq(new).astype(dtype)
        if past is not None:
            new = jnp.concatenate([cached.reshape(b, pl_len, -1).astype(dtype), new], axis=1)
        return jnp.pad(new, ((0, 0), (0, s_pad - s_valid), (0, 0)))

    rows = (seq(fk_f).reshape(b, t, FOX_HEADS, FOX_DIM), seq(fv_f).reshape(b, t, FOX_HEADS, FOX_DIM),
            seq(lf8)[:, :, :FOX_HEADS], seq(bk_f), seq(bv_f), seq(ik_f), seq(ckv_f), seq(kpe_f))
    cache = (None,) * 8 if past is None else past

    lf_all = with_past(None if past is None else jnp.pad(cache[2], ((0, 0), (0, 0), (0, 2))), lf8, F32)
    cum_t = _cumsum_call(jnp.swapaxes(lf_all, 1, 2))
    ck = cum_t[:, :FOX_HEADS].reshape(b, FOX_HEADS // 2, 2, s_pad)
    cq = jnp.swapaxes(ck[:, :, :, pl_len:pl_len + t], 2, 3)
    ya = _fox_call(seq(fq), with_past(cache[0], fk_b, BF16), with_past(cache[1], fv_b, BF16),
                   cq, ck, tq, tk, pl_len)

    def rep(a, k):
        return jnp.tile(a.reshape(b, pl_len, -1), (1, 1, k))

    bias_t, far = _bias_tables(rel_bias, tq_dsa)
    pad_q = lambda a: jnp.pad(a, ((0, 0), (0, t_dsa - t), (0, 0)))
    ik_all = with_past(None if past is None else rep(cache[5], IDX_HEADS), ik_r, BF16)
    bk_all = with_past(None if past is None else rep(cache[3], DSA_HEADS), bk_r, BF16)
    bv_t = jnp.swapaxes(with_past(cache[4], bv_b, BF16), 1, 2)
    iw_t = jnp.swapaxes(pad_q(seq(iw8)), 1, 2)
    yb = _dsa_call(far, pad_q(seq(iq)), iw_t, ik_all, pad_q(seq(bq)), bk_all, bv_t, bias_t,
                   tq_dsa, pl_len, s_valid, k_sel)[:, :t]

    ckv_all = with_past(cache[6], ckv_f, F32).reshape(b * s_pad, MLA_KV_LORA)
    kpe_all = with_past(cache[7], kpe_f, BF16)
    kpe_all = jnp.pad(kpe_all, ((0, 0), (0, 0), (0, LANE - MLA_ROPE))).reshape(b * s_pad, LANE)
    kc, mv = _kvup_call(ckv_all, kpe_all, kvup_c, min(512, b * s_pad))
    yc = _mla_call(seq(qc), kc.reshape(b, s_pad, -1), mv.reshape(b, s_pad, -1), tq, tk, pl_len, s_valid)

    x1 = _merge_call(x2d, ya.reshape(n, -1), yb.reshape(n, -1), yc.reshape(n, -1), merge_c, tm)
    x2 = _ffn_call(x1, *ffn_c, tm)
    return x2.reshape(b, t, d), rows


def kernel(x_prompt, x_sample, cache_fox_k, cache_fox_v, cache_fox_logf, cache_dsa_k, cache_dsa_v, cache_idx_k, cache_mla_ckv, cache_mla_kpe, rel_bias, norm_mix, w_in, fox_gq, fox_gk, fox_bf, dsa_gq, dsa_gk, idx_gk, mla_gqa, mla_wqb, mla_gqn, mla_gqr, mla_gkv, mla_gkr, mla_wkvb, mla_gkn, w_fox_out, w_dsa_out, w_mla_out, w_gate, w_o, norm_ffn, w_ffn_in, w_ffn_out):
    caches = (cache_fox_k, cache_fox_v, cache_fox_logf, cache_dsa_k, cache_dsa_v,
              cache_idx_k, cache_mla_ckv, cache_mla_kpe)
    yp, ys = x_prompt, x_sample
    p_rows, s_rows = [], []
    for i in range(norm_mix.shape[0]):
        p = dict(norm_mix=norm_mix[i], w_in=w_in[i], fox_gq=fox_gq[i], fox_gk=fox_gk[i],
                 fox_bf=fox_bf[i], dsa_gq=dsa_gq[i], dsa_gk=dsa_gk[i], idx_gk=idx_gk[i],
                 mla_gqa=mla_gqa[i], mla_wqb=mla_wqb[i], mla_gqn=mla_gqn[i], mla_gqr=mla_gqr[i],
                 mla_gkv=mla_gkv[i], mla_gkr=mla_gkr[i], mla_wkvb=mla_wkvb[i], mla_gkn=mla_gkn[i],
                 w_fox_out=w_fox_out[i], w_dsa_out=w_dsa_out[i], w_mla_out=w_mla_out[i],
                 w_gate=w_gate[i], w_o=w_o[i], norm_ffn=norm_ffn[i], w_ffn_in=w_ffn_in[i],
                 w_ffn_out=w_ffn_out[i])
        yp, rows_p = _layer(yp, None, p, rel_bias)
        ys, rows_s = _layer(ys, tuple(c[i] for c in caches), p, rel_bias)
        p_rows.append(rows_p)
        s_rows.append(rows_s)

    def st(rows, j):
        return jnp.stack([r[j] for r in rows], axis=0)

    return ((yp, ys) + tuple(st(p_rows, j) for j in range(8)) + tuple(st(s_rows, j) for j in range(8)))
```

```python
import functools
import math

import numpy as np
import jax
import jax.numpy as jnp
from jax import lax
from jax.experimental import pallas as pl
from jax.experimental.pallas import tpu as pltpu

F32 = jnp.float32
BF16 = jnp.bfloat16

CHUNK = 64
EPS = 1e-6
FOX_HEADS, FOX_DIM = 6, 64
FOX_W = FOX_HEADS * FOX_DIM
DSA_HEADS, DSA_DIM = 4, 64
DSA_W = DSA_HEADS * DSA_DIM
IDX_HEADS, IDX_DIM = 8, 32
IDX_W = IDX_HEADS * IDX_DIM
TOPK_MAX = 256
MLA_HEADS = 6
MLA_Q_LORA, MLA_KV_LORA = 256, 128
MLA_NOPE, MLA_ROPE, MLA_V = 64, 32, 64
MLA_W = MLA_HEADS * MLA_V
ROPE_THETA = 10000.0
REL_BUCKETS, REL_MAX_DIST = 32, 128

LANE = 128
KB_SMALL, KB_BIG = 256, 512
BIAS_BACK = 768
BIAS_ROWS = BIAS_BACK + KB_BIG
LOG2E = math.log2(math.e)
M_INIT = -1e30
MASKED = -3e30
INT_MIN = -2 ** 31
KEY_NEG_INF = -2139095041
VMEM_LIMIT = 56 * 1024 * 1024

_C_FQ, _C_FK, _C_FV, _C_FG = 0, 384, 768, 1152
_C_BQ, _C_BK, _C_BV = 1280, 1536, 1792
_C_IQ, _C_IK, _C_IW = 1920, 2176, 2432
_C_QA, _C_KV, _C_PE = 2560, 2816, 2944
_C_END = 3072


def _dot(a, b):
    return jnp.dot(a, b, preferred_element_type=F32)


def _dot_nt(a, b):
    return lax.dot_general(a, b, (((1,), (1,)), ((), ())), preferred_element_type=F32)


def _dot_hilo(x, m):
    hi = x.astype(BF16)
    lo = (x - hi.astype(F32)).astype(BF16)
    return _dot(hi, m) + _dot(lo, m)


def _rms_rows(x, g):
    return x * lax.rsqrt(jnp.mean(x * x, axis=-1, keepdims=True) + EPS) * g


def _rms_groups(x, avg, g):
    return x * lax.rsqrt(_dot_hilo(x * x, avg) + EPS) * g


def _params(sem, vmem=VMEM_LIMIT):
    return pltpu.CompilerParams(dimension_semantics=sem, vmem_limit_bytes=vmem)


def _const_spec(shape):
    nd = len(shape)
    return pl.BlockSpec(shape, lambda *_: (0,) * nd)


def _proj_kernel(x_ref, gmix_ref, w_ref, bf_ref, gfq_ref, gfk_ref, gbq_ref, gbk_ref, gik_ref,
                 gqa_ref, gkv_ref, gkr_ref, wqb_ref, gqc_ref,
                 a64x6_ref, a64x4_ref, a32x8_ref, aqc_ref, ape_ref, rq_ref, rk_ref,
                 cosq_ref, sinq_ref, cosk_ref, sink_ref,
                 fq_o, fkf_o, fkb_o, fvf_o, fvb_o, lf_o, bq_o, bkf_o, bkr_o, bvf_o, bvb_o,
                 iq_o, ikf_o, ikr_o, iw_o, qc_o, ckv_o, kpe_o):
    x = x_ref[...]
    hb = _rms_rows(x, gmix_ref[...]).astype(BF16)

    def grp(lo, hi):
        return _dot(hb, w_ref[:, lo:hi])

    fq_o[...] = _rms_groups(grp(_C_FQ, _C_FK), a64x6_ref[...], gfq_ref[...]).astype(BF16)
    fk = _rms_groups(grp(_C_FK, _C_FV), a64x6_ref[...], gfk_ref[...])
    fkf_o[...] = fk
    fkb_o[...] = fk.astype(BF16)
    fv = grp(_C_FV, _C_FG)
    fvf_o[...] = fv
    fvb_o[...] = fv.astype(BF16)
    z = grp(_C_FG, _C_BQ) + bf_ref[...]
    lf = jnp.minimum(z, 0.0) - jnp.log1p(jnp.exp(-jnp.abs(z)))
    lf_o[...] = lf[:, :8]

    bq_o[...] = _rms_groups(grp(_C_BQ, _C_BK), a64x4_ref[...], gbq_ref[...]).astype(BF16)
    bk = _rms_groups(grp(_C_BK, _C_BV), a64x4_ref[...], gbk_ref[...])
    bkf_o[...] = bk[:, :DSA_DIM]
    bkr_o[...] = bk.astype(BF16)
    bv = grp(_C_BV, _C_IQ)[:, :DSA_DIM]
    bvf_o[...] = bv
    bvb_o[...] = bv.astype(BF16)
    iq_o[...] = grp(_C_IQ, _C_IK).astype(BF16)
    ik = _rms_groups(grp(_C_IK, _C_IW), a32x8_ref[...], gik_ref[...])
    ikf_o[...] = ik[:, :IDX_DIM]
    ikr_o[...] = ik.astype(BF16)
    iw_o[...] = (grp(_C_IW, _C_QA) * (1.0 / 16.0))[:, :8]

    cq = _rms_rows(grp(_C_QA, _C_KV), gqa_ref[...]).astype(BF16)
    cosq, sinq = cosq_ref[...], sinq_ref[...]
    for h in range(MLA_HEADS):
        sl = slice(h * LANE, (h + 1) * LANE)
        qh = _rms_groups(_dot(cq, wqb_ref[:, sl]), aqc_ref[...], gqc_ref[:, sl])
        qc_o[:, sl] = (qh * cosq + _dot_hilo(qh, rq_ref[...]) * sinq).astype(BF16)
    ckv_o[...] = _rms_rows(grp(_C_KV, _C_PE), gkv_ref[...])
    kp = _rms_groups(grp(_C_PE, _C_END), ape_ref[...], gkr_ref[...])
    kp = kp * cosk_ref[...] + _dot_hilo(kp, rk_ref[...]) * sink_ref[...]
    kpe_o[...] = kp[:, :MLA_ROPE]


def _proj_call(x2d, consts, tabs, tm, n_tab_blocks):
    n = x2d.shape[0]
    row = lambda w: pl.BlockSpec((tm, w), lambda i: (i, 0))
    tab = pl.BlockSpec((tm, LANE), lambda i: (i % n_tab_blocks, 0))
    in_specs = [row(x2d.shape[1])] + [_const_spec(c.shape) for c in consts] + [tab] * 4
    widths = [(FOX_W, BF16), (FOX_W, F32), (FOX_W, BF16), (FOX_W, F32), (FOX_W, BF16), (8, F32),
              (DSA_W, BF16), (DSA_DIM, F32), (DSA_W, BF16), (DSA_DIM, F32), (DSA_DIM, BF16),
              (IDX_W, BF16), (IDX_DIM, F32), (IDX_W, BF16), (8, F32),
              (MLA_HEADS * LANE, BF16), (MLA_KV_LORA, F32), (MLA_ROPE, F32)]
    return pl.pallas_call(
        _proj_kernel,
        grid=(n // tm,),
        in_specs=in_specs,
        out_specs=[row(w) for w, _ in widths],
        out_shape=[jax.ShapeDtypeStruct((n, w), d) for w, d in widths],
        compiler_params=_params(("parallel",)),
        name="proj",
    )(x2d, *consts, *tabs)


def _cumsum_kernel(x_ref, tri_ref, o_ref):
    nblk = x_ref.shape[2] // LANE

    def body(j, carry):
        s = pl.multiple_of(j * LANE, LANE)
        x = x_ref[0, :, pl.ds(s, LANE)]
        x1 = x.astype(BF16)
        r1 = x - x1.astype(F32)
        x2 = r1.astype(BF16)
        x3 = (r1 - x2.astype(F32)).astype(BF16)
        tri = tri_ref[...]
        out = _dot(x1, tri) + _dot(x2, tri) + _dot(x3, tri) + carry
        o_ref[0, :, pl.ds(s, LANE)] = out * LOG2E
        return out[:, LANE - 1:LANE]

    lax.fori_loop(0, nblk, body, jnp.zeros((8, 1), F32))


def _cumsum_call(lf_t):
    b, _, s = lf_t.shape
    tri = jnp.asarray(np.triu(np.ones((LANE, LANE), np.float32)), BF16)
    return pl.pallas_call(
        _cumsum_kernel,
        grid=(b,),
        in_specs=[pl.BlockSpec((1, 8, s), lambda i: (i, 0, 0)), _const_spec((LANE, LANE))],
        out_specs=pl.BlockSpec((1, 8, s), lambda i: (i, 0, 0)),
        out_shape=jax.ShapeDtypeStruct((b, 8, s), F32),
        compiler_params=_params(("parallel",)),
        name="cumsum",
    )(lf_t, tri)


def _softmax_steps(states, logits, vts):
    mid = []
    for (m, l, acc), s in zip(states, logits):
        m_new = jnp.maximum(m, jnp.max(s, axis=0, keepdims=True))
        alpha = jnp.exp2(m - m_new)
        p = jnp.exp2(s - m_new)
        mid.append((m_new, alpha * l + jnp.sum(p, axis=0, keepdims=True), alpha * acc, p.astype(BF16)))
    return tuple((m_new, l, acc + _dot(vt, p)) for (m_new, l, acc, p), vt in zip(mid, vts))


def _init_state(n_heads, tq):
    return tuple((jnp.full((1, tq), M_INIT, F32), jnp.zeros((1, tq), F32), jnp.zeros((64, tq), F32))
                 for _ in range(n_heads))


def _finish(state, o_ref):
    out_t = jnp.concatenate([acc / l for _, l, acc in state], axis=0)
    o_ref[0] = out_t.T.astype(o_ref.dtype)


def _sweep(n_small, step, carry):
    ratio = KB_BIG // KB_SMALL
    n_big = n_small // ratio

    def big(j, c):
        return step(pl.multiple_of(j * KB_BIG, KB_BIG), KB_BIG, c)

    def small(j, c):
        return step(pl.multiple_of(j * KB_SMALL, KB_SMALL), KB_SMALL, c)

    carry = lax.fori_loop(0, n_big, big, carry)
    return lax.fori_loop(n_big * ratio, n_small, small, carry)


def _fox_kernel(q_ref, k_ref, vt_ref, cq_ref, ck_ref, o_ref, *, tq, past):
    q0 = past + pl.program_id(1) * tq
    lane = lax.broadcasted_iota(jnp.int32, (tq, LANE), 1)
    q_heads, cq = [], []
    for h in range(FOX_HEADS):
        qp = q_ref[0, :, (h // 2) * LANE:(h // 2 + 1) * LANE]
        q_heads.append(jnp.where((lane < 64) == (h % 2 == 0), qp, jnp.zeros_like(qp)))
        cq.append(cq_ref[0, h:h + 1, :])

    def step(ks, width, state, masked=False):
        if masked:
            ok = (ks + lax.broadcasted_iota(jnp.int32, (width, tq), 0)
                  <= q0 + lax.broadcasted_iota(jnp.int32, (width, tq), 1))
        logits = []
        for h in range(FOX_HEADS):
            kb = k_ref[0, pl.ds(ks, width), (h // 2) * LANE:(h // 2 + 1) * LANE]
            s = _dot_nt(kb, q_heads[h]) + cq[h] - ck_ref[0, pl.ds(ks, width), h:h + 1]
            logits.append(jnp.where(ok, s, MASKED) if masked else s)
        vts = [vt_ref[0, h * 64:(h + 1) * 64, pl.ds(ks, width)] for h in range(FOX_HEADS)]
        return _softmax_steps(state, logits, vts)

    n_free = q0 // KB_SMALL
    state = _sweep(n_free, step, _init_state(FOX_HEADS, tq))
    state = step(pl.multiple_of(n_free * KB_SMALL, KB_SMALL), KB_SMALL, state, masked=True)
    _finish(state, o_ref)


def _fox_call(q, k, vt, cq, ck, tq, past):
    b, t, _ = q.shape
    s = k.shape[1]
    return pl.pallas_call(
        functools.partial(_fox_kernel, tq=tq, past=past),
        grid=(b, t // tq),
        in_specs=[
            pl.BlockSpec((1, tq, FOX_W), lambda i, j: (i, j, 0)),
            pl.BlockSpec((1, s, FOX_W), lambda i, j: (i, 0, 0)),
            pl.BlockSpec((1, FOX_W, s), lambda i, j: (i, 0, 0)),
            pl.BlockSpec((1, 8, tq), lambda i, j: (i, 0, j)),
            pl.BlockSpec((1, s, 8), lambda i, j: (i, 0, 0)),
        ],
        out_specs=pl.BlockSpec((1, tq, FOX_W), lambda i, j: (i, j, 0)),
        out_shape=jax.ShapeDtypeStruct((b, t, FOX_W), BF16),
        compiler_params=_params(("parallel", "parallel")),
        name="fox",
    )(q, k, vt, cq, ck)


def _mla_kernel(q_ref, k_ref, vt_ref, o_ref, *, tq, past, s_valid):
    q0 = past + pl.program_id(1) * tq
    q_heads = [q_ref[0, :, h * LANE:(h + 1) * LANE] for h in range(MLA_HEADS)]

    def step(ks, width, state, masked=False):
        if masked:
            kpos = ks + lax.broadcasted_iota(jnp.int32, (width, tq), 0)
            qchunk = (q0 + lax.broadcasted_iota(jnp.int32, (width, tq), 1)) // CHUNK
            ok = (kpos // CHUNK <= qchunk) & (kpos < s_valid)
        logits = []
        for h in range(MLA_HEADS):
            s = _dot_nt(k_ref[0, pl.ds(ks, width), h * LANE:(h + 1) * LANE], q_heads[h])
            logits.append(jnp.where(ok, s, MASKED) if masked else s)
        vts = [vt_ref[0, h * 64:(h + 1) * 64, pl.ds(ks, width)] for h in range(MLA_HEADS)]
        return _softmax_steps(state, logits, vts)

    n_free = q0 // KB_SMALL
    state = _sweep(n_free, step, _init_state(MLA_HEADS, tq))
    state = step(pl.multiple_of(n_free * KB_SMALL, KB_SMALL), KB_SMALL, state, masked=True)
    _finish(state, o_ref)


def _mla_call(q, k, vt, tq, past, s_valid):
    b, t, _ = q.shape
    s = k.shape[1]
    return pl.pallas_call(
        functools.partial(_mla_kernel, tq=tq, past=past, s_valid=s_valid),
        grid=(b, t // tq),
        in_specs=[
            pl.BlockSpec((1, tq, MLA_HEADS * LANE), lambda i, j: (i, j, 0)),
            pl.BlockSpec((1, s, MLA_HEADS * LANE), lambda i, j: (i, 0, 0)),
            pl.BlockSpec((1, MLA_W, s), lambda i, j: (i, 0, 0)),
        ],
        out_specs=pl.BlockSpec((1, tq, MLA_W), lambda i, j: (i, j, 0)),
        out_shape=jax.ShapeDtypeStruct((b, t, MLA_W), BF16),
        compiler_params=_params(("parallel", "parallel")),
        name="mla",
    )(q, k, vt)


def _kvup_kernel(ckv_ref, kpe_ref, wk_ref, wv_ref, gkn_ref, akn_ref, place_ref, k_o, v_o):
    cb = ckv_ref[...].astype(BF16)
    v_o[...] = _dot(cb, wv_ref[...]).astype(BF16)
    pe = _dot(kpe_ref[...], place_ref[...])
    for h in range(MLA_HEADS):
        sl = slice(h * LANE, (h + 1) * LANE)
        kn = _rms_groups(_dot(cb, wk_ref[:, sl]), akn_ref[...], gkn_ref[...])
        k_o[:, sl] = (kn + pe).astype(BF16)


def _kvup_call(ckv2d, kpe2d, consts, tm):
    n = ckv2d.shape[0]
    row = lambda w: pl.BlockSpec((tm, w), lambda i: (i, 0))
    return pl.pallas_call(
        _kvup_kernel,
        grid=(n // tm,),
        in_specs=[row(MLA_KV_LORA), row(LANE)] + [_const_spec(c.shape) for c in consts],
        out_specs=[row(MLA_HEADS * LANE), row(MLA_W)],
        out_shape=[jax.ShapeDtypeStruct((n, MLA_HEADS * LANE), BF16),
                   jax.ShapeDtypeStruct((n, MLA_W), BF16)],
        compiler_params=_params(("parallel",)),
        name="kvup",
    )(ckv2d, kpe2d, *consts)


def _dsa_kernel(far_ref, iq_ref, iw_ref, ik_ref, bq_ref, bk_ref, bvt_ref, bias_ref, low_ref,
                o_ref, keys_ref, *, tq, past, s_valid, k_sel):
    kb = KB_BIG
    q0 = past + pl.program_id(1) * tq
    nkb = (jnp.minimum(q0 + tq, s_valid) + kb - 1) // kb
    n_far = jnp.maximum(nkb - 2, 0)
    qpos = q0 + lax.broadcasted_iota(jnp.int32, (kb, tq), 1)
    koff = lax.broadcasted_iota(jnp.int32, (kb, tq), 0)
    qlane = lax.broadcasted_iota(jnp.int32, (tq, IDX_W), 1)

    iq = iq_ref[0]
    iq_heads = [jnp.where(qlane // IDX_DIM == h, iq, jnp.zeros_like(iq)) for h in range(IDX_HEADS)]

    def score_body(j, _):
        ks = pl.multiple_of(j * kb, kb)
        ikb = ik_ref[0, pl.ds(ks, kb), :]
        acc = jnp.zeros((kb, tq), F32)
        for h in range(IDX_HEADS):
            acc = acc + jnp.maximum(_dot_nt(ikb, iq_heads[h]), 0.0) * iw_ref[0, h:h + 1, :]
        kpos = ks + koff
        adm = (kpos // CHUNK <= qpos // CHUNK) & (kpos < s_valid)
        bits = pltpu.bitcast(acc, jnp.int32)
        key = jnp.where(bits < 0, bits ^ 0x7FFFFFFF, bits)
        keys_ref[pl.ds(ks, kb), :] = jnp.where(adm, key, KEY_NEG_INF)
        return 0

    lax.fori_loop(0, nkb, score_body, 0)

    def count_ge(cand):
        def body(j, c8):
            ks = pl.multiple_of(j * kb, kb)
            hit = jnp.where(keys_ref[pl.ds(ks, kb), :] >= cand, 1, 0)
            return c8 + jnp.sum(hit.reshape(kb // 8, 8, tq), axis=0)
        c8 = lax.fori_loop(0, nkb, body, jnp.zeros((8, tq), jnp.int32))
        return jnp.sum(c8, axis=0, keepdims=True)

    def search_cond(st):
        i, _, done = st
        return (i < 32) & (jnp.min(done) == 0)

    def search_body(st):
        i, ub, done = st
        c = ub | jnp.left_shift(jnp.int32(1), 31 - i)
        cnt = count_ge(c ^ INT_MIN)
        return i + 1, jnp.where(cnt >= k_sel, c, ub), jnp.where(cnt == k_sel, 1, done)

    cnt0 = count_ge(jnp.zeros((1, tq), jnp.int32))
    done0 = (cnt0 == k_sel) | ((cnt0 > k_sel) & (count_ge(jnp.ones((1, tq), jnp.int32)) < k_sel))
    ub0 = jnp.where(cnt0 >= k_sel, INT_MIN, 0)
    _, ub, _ = lax.while_loop(search_cond, search_body, (jnp.int32(1), ub0, jnp.where(done0, 1, 0)))
    thr = ub ^ INT_MIN
    need = (k_sel - count_ge(thr + 1)).astype(F32)
    thr_next = thr + 1

    bq = bq_ref[0]
    blane = lax.broadcasted_iota(jnp.int32, (tq, DSA_W), 1)
    bq_heads = [jnp.where(blane // DSA_DIM == h, bq, jnp.zeros_like(bq)) for h in range(DSA_HEADS)]

    def att_body(j, carry, near):
        seen, state = carry
        ks = pl.multiple_of(j * kb, kb)
        key = keys_ref[pl.ds(ks, kb), :]
        tie = jnp.where(key == thr, 1.0, 0.0)
        rank = seen + _dot(low_ref[...], tie.astype(BF16))
        sel = key >= jnp.maximum(jnp.where(rank < need, thr, thr_next), KEY_NEG_INF + 1)
        bkb = bk_ref[0, pl.ds(ks, kb), :]
        vt = bvt_ref[0, :, pl.ds(ks, kb)]
        if near:
            bs = pl.multiple_of(ks - (q0 - BIAS_BACK), KB_SMALL)
        logits = []
        for h in range(DSA_HEADS):
            s = _dot_nt(bkb, bq_heads[h])
            s = s + (bias_ref[h, pl.ds(bs, kb), :] if near else far_ref[h])
            logits.append(jnp.where(sel, s, MASKED))
        return (seen + jnp.sum(tie, axis=0, keepdims=True),
                _softmax_steps(state, logits, [vt] * DSA_HEADS))

    carry = (jnp.zeros((1, tq), F32), _init_state(DSA_HEADS, tq))
    carry = lax.fori_loop(0, n_far, functools.partial(att_body, near=False), carry)
    carry = lax.fori_loop(n_far, nkb, functools.partial(att_body, near=True), carry)
    _finish(carry[1], o_ref)


def _dsa_call(far, iq, iw_t, ik, bq, bk, bv_t, bias_t, tq, past, s_valid, k_sel):
    b, t, _ = iq.shape
    s = ik.shape[1]
    low = jnp.asarray(np.tril(np.ones((KB_BIG, KB_BIG), np.float32), -1), BF16)
    return pl.pallas_call(
        functools.partial(_dsa_kernel, tq=tq, past=past, s_valid=s_valid, k_sel=k_sel),
        grid=(b, t // tq),
        in_specs=[
            pl.BlockSpec(memory_space=pltpu.SMEM),
            pl.BlockSpec((1, tq, IDX_W), lambda i, j: (i, j, 0)),
            pl.BlockSpec((1, 8, tq), lambda i, j: (i, 0, j)),
            pl.BlockSpec((1, s, IDX_W), lambda i, j: (i, 0, 0)),
            pl.BlockSpec((1, tq, DSA_W), lambda i, j: (i, j, 0)),
            pl.BlockSpec((1, s, DSA_W), lambda i, j: (i, 0, 0)),
            pl.BlockSpec((1, DSA_DIM, s), lambda i, j: (i, 0, 0)),
            _const_spec(bias_t.shape),
            _const_spec(low.shape),
        ],
        out_specs=pl.BlockSpec((1, tq, DSA_W), lambda i, j: (i, j, 0)),
        out_shape=jax.ShapeDtypeStruct((b, t, DSA_W), BF16),
        scratch_shapes=[pltpu.VMEM((s, tq), jnp.int32)],
        compiler_params=_params(("parallel", "parallel")),
        name="dsa",
    )(far, iq, iw_t, ik, bq, bk, bv_t, bias_t, low)


def _merge_kernel(x_ref, ya_ref, yb_ref, yc_ref, g_ref, wg_ref, wa_ref, wb_ref, wc_ref, wo_ref, o_ref):
    x = x_ref[...]
    d = x.shape[1]
    hb = _rms_rows(x, g_ref[...]).astype(BF16)
    mix = jnp.zeros_like(x)
    for i, (y_ref, w_ref) in enumerate(((ya_ref, wa_ref), (yb_ref, wb_ref), (yc_ref, wc_ref))):
        gate = jax.nn.sigmoid(_dot(hb, wg_ref[:, i * d:(i + 1) * d]))
        mix = mix + gate * _dot(y_ref[...], w_ref[...])
    o_ref[...] = x + _dot(mix.astype(BF16), wo_ref[...])


def _merge_call(x2d, ya, yb, yc, consts, tm):
    n, d = x2d.shape
    row = lambda w: pl.BlockSpec((tm, w), lambda i: (i, 0))
    return pl.pallas_call(
        _merge_kernel,
        grid=(n // tm,),
        in_specs=[row(d), row(FOX_W), row(DSA_W), row(MLA_W)] + [_const_spec(c.shape) for c in consts],
        out_specs=row(d),
        out_shape=jax.ShapeDtypeStruct((n, d), F32),
        compiler_params=_params(("parallel",)),
        name="merge",
    )(x2d, ya, yb, yc, *consts)


def _ffn_kernel(x_ref, g_ref, wi_ref, wo_ref, o_ref, *, n_chunks):
    x = x_ref[...]
    hb = _rms_rows(x, g_ref[...]).astype(BF16)
    out = x
    for c in range(n_chunks):
        gu = _dot(hb, wi_ref[c])
        half = gu.shape[1] // 2
        gt, up = gu[:, :half], gu[:, half:]
        out = out + _dot((gt * jax.nn.sigmoid(gt) * up).astype(BF16), wo_ref[c])
    o_ref[...] = out


def _ffn_call(x2d, g, wi, wo, tm):
    n, d = x2d.shape
    row = pl.BlockSpec((tm, d), lambda i: (i, 0))
    return pl.pallas_call(
        functools.partial(_ffn_kernel, n_chunks=wi.shape[0]),
        grid=(n // tm,),
        in_specs=[row, _const_spec(g.shape), _const_spec(wi.shape), _const_spec(wo.shape)],
        out_specs=row,
        out_shape=jax.ShapeDtypeStruct((n, d), F32),
        compiler_params=_params(("parallel",)),
        name="ffn",
    )(x2d, g, wi, wo)


def _block_avg(blocks, width):
    m = np.zeros((width, width), np.float32)
    for lo, hi in blocks:
        m[lo:hi, lo:hi] = 1.0 / (hi - lo)
    return jnp.asarray(m, BF16)


def _rot_matrix(base):
    r = np.zeros((LANE, LANE), np.float32)
    half = MLA_ROPE // 2
    for i in range(half):
        r[base + half + i, base + i] = -1.0
        r[base + i, base + half + i] = 1.0
    return jnp.asarray(r, BF16)


def _rope_tables(pos, base):
    half = MLA_ROPE // 2
    freq = ROPE_THETA ** (-jnp.arange(half, dtype=F32) / half)
    ang = pos.astype(F32)[:, None] * freq[None, :]
    cos, sin = jnp.cos(ang), jnp.sin(ang)
    n = pos.shape[0]
    ct = jnp.ones((n, LANE), F32).at[:, base:base + MLA_ROPE].set(jnp.concatenate([cos, cos], 1))
    st = jnp.zeros((n, LANE), F32).at[:, base:base + MLA_ROPE].set(jnp.concatenate([sin, sin], 1))
    return ct, st


def _t5_bucket(rel):
    nb = REL_BUCKETS // 2
    max_exact = nb // 2
    side = jnp.where(rel > 0, nb, 0)
    n = jnp.abs(rel)
    large = max_exact + (jnp.log(jnp.maximum(n, 1).astype(F32) / max_exact)
                         / math.log(REL_MAX_DIST / max_exact) * (nb - max_exact)).astype(jnp.int32)
    large = jnp.minimum(large, nb - 1)
    return side + jnp.where(n < max_exact, n, large)


def _bias_tables(rel_bias, tq):
    c = jnp.arange(BIAS_ROWS, dtype=jnp.int32)[:, None]
    r = jnp.arange(tq, dtype=jnp.int32)[None, :]
    near = jnp.moveaxis(rel_bias[_t5_bucket(c - BIAS_BACK - r)], 2, 0).astype(F32) * LOG2E
    far = rel_bias[_t5_bucket(jnp.int32(-REL_MAX_DIST))].astype(F32) * LOG2E
    return near, far


def _pad_cols(w, width):
    return jnp.pad(w, ((0, 0), (0, width - w.shape[1])))


def _tile_vec(g, reps, scale=1.0):
    return (jnp.tile(g.astype(F32), reps) * scale)[None, :]


def _layer_consts(p):
    splits = np.cumsum([FOX_W, FOX_W, FOX_W, FOX_HEADS, DSA_W, DSA_DIM, DSA_DIM, IDX_W, IDX_DIM, IDX_HEADS,
                        MLA_Q_LORA, MLA_KV_LORA, MLA_ROPE])[:-1]
    fq, fk, fv, fg, bq, bk, bv, iq, ik, iw, cqa, ckva, ckpe = jnp.split(p['w_in'], [int(v) for v in splits], axis=1)
    w_all = jnp.concatenate([
        fq, fk, fv, _pad_cols(fg, LANE), bq, jnp.tile(bk, (1, DSA_HEADS)), _pad_cols(bv, LANE),
        iq, jnp.tile(ik, (1, IDX_HEADS)), _pad_cols(iw, LANE), cqa, ckva, _pad_cols(ckpe, LANE)],
        axis=1).astype(BF16)
    assert w_all.shape[1] == _C_END
    d_qk = MLA_NOPE + MLA_ROPE
    wqb = p['mla_wqb'].reshape(MLA_Q_LORA, MLA_HEADS, d_qk)
    wqb = jnp.pad(wqb, ((0, 0), (0, 0), (0, LANE - d_qk))).reshape(MLA_Q_LORA, MLA_HEADS * LANE).astype(BF16)
    gqc = jnp.concatenate([p['mla_gqn'], p['mla_gqr'], jnp.zeros((LANE - d_qk,), F32)]).astype(F32)
    proj = [
        p['norm_mix'].astype(F32)[None, :], w_all,
        _pad_cols(p['fox_bf'].astype(F32)[None, :], LANE),
        _tile_vec(p['fox_gq'], FOX_HEADS, FOX_DIM ** -0.5 * LOG2E), _tile_vec(p['fox_gk'], FOX_HEADS),
        _tile_vec(p['dsa_gq'], DSA_HEADS, DSA_DIM ** -0.5 * LOG2E), _tile_vec(p['dsa_gk'], DSA_HEADS),
        _tile_vec(p['idx_gk'], IDX_HEADS),
        p['mla_gqa'].astype(F32)[None, :], p['mla_gkv'].astype(F32)[None, :],
        _pad_cols(p['mla_gkr'].astype(F32)[None, :], LANE),
        wqb, _tile_vec(gqc, MLA_HEADS, d_qk ** -0.5 * LOG2E),
        _block_avg([(i * 64, i * 64 + 64) for i in range(6)], FOX_W),
        _block_avg([(i * 64, i * 64 + 64) for i in range(4)], DSA_W),
        _block_avg([(i * 32, i * 32 + 32) for i in range(8)], IDX_W),
        _block_avg([(0, MLA_NOPE), (MLA_NOPE, d_qk)], LANE),
        _block_avg([(0, MLA_ROPE)], LANE),
        _rot_matrix(MLA_NOPE), _rot_matrix(0),
    ]
    wkvb = p['mla_wkvb'].reshape(MLA_KV_LORA, MLA_HEADS, MLA_NOPE + MLA_V)
    wk = jnp.pad(wkvb[:, :, :MLA_NOPE], ((0, 0), (0, 0), (0, LANE - MLA_NOPE)))
    wk = wk.reshape(MLA_KV_LORA, MLA_HEADS * LANE).astype(BF16)
    wv = wkvb[:, :, MLA_NOPE:].reshape(MLA_KV_LORA, MLA_W).astype(BF16)
    place = np.zeros((LANE, LANE), np.float32)
    place[np.arange(MLA_ROPE), MLA_NOPE + np.arange(MLA_ROPE)] = 1.0
    kvup = [wk, wv, _pad_cols(p['mla_gkn'].astype(F32)[None, :], LANE),
            _block_avg([(0, MLA_NOPE)], LANE), jnp.asarray(place, BF16)]
    merge = [p['norm_mix'].astype(F32)[None, :], p['w_gate'].astype(BF16), p['w_fox_out'].astype(BF16),
             p['w_dsa_out'].astype(BF16), p['w_mla_out'].astype(BF16), p['w_o'].astype(BF16)]
    d_ff = p['w_ffn_out'].shape[0]
    n_chunks = 2
    ck = d_ff // n_chunks
    wi = p['w_ffn_in']
    wi = jnp.stack([jnp.concatenate([wi[:, c * ck:(c + 1) * ck], wi[:, d_ff + c * ck:d_ff + (c + 1) * ck]], 1)
                    for c in range(n_chunks)]).astype(BF16)
    wo = p['w_ffn_out'].reshape(n_chunks, ck, -1).astype(BF16)
    ffn = [p['norm_ffn'].astype(F32)[None, :], wi, wo]
    return proj, kvup, merge, ffn


def _round_up(v, m):
    return -(-v // m) * m


def _layer(x, past, p, rel_bias):
    b, t, d = x.shape
    pl_len = 0 if past is None else past[0].shape[1]
    s_valid = pl_len + t
    n = b * t
    tm = min(512, n)
    t_att = _round_up(t, LANE)
    tq = min(256, t_att)
    assert pl_len % KB_SMALL == 0 and t_att % tq == 0
    s_pad = _round_up(pl_len + t_att - tq + max(tq, KB_SMALL), KB_BIG)
    k_sel = min(TOPK_MAX, s_valid // 4)
    proj_c, kvup_c, merge_c, ffn_c = _layer_consts(p)

    pos = pl_len + jnp.arange(t, dtype=jnp.int32)
    tabs = _rope_tables(pos, MLA_NOPE) + _rope_tables(pos, 0)
    if t < tm:
        tabs = tuple(jnp.tile(a, (tm // t, 1)) for a in tabs)
    n_tab_blocks = max(t // tm, 1)

    x2d = x.reshape(n, d)
    (fq, fk_f, fk_b, fv_f, fv_b, lf8, bq, bk_f, bk_r, bv_f, bv_b, iq, ik_f, ik_r, iw8, qc, ckv_f, kpe_f
     ) = _proj_call(x2d, proj_c, tabs, tm, n_tab_blocks)

    def seq(a):
        return a.reshape(b, t, a.shape[-1])

    def queries(a):
        return jnp.pad(seq(a), ((0, 0), (0, t_att - t), (0, 0)))

    def with_past(cached, new, dtype):
        new = seq(new).astype(dtype)
        if past is not None:
            new = jnp.concatenate([cached.reshape(b, pl_len, -1).astype(dtype), new], axis=1)
        return jnp.pad(new, ((0, 0), (0, s_pad - s_valid), (0, 0)))

    rows = (seq(fk_f).reshape(b, t, FOX_HEADS, FOX_DIM), seq(fv_f).reshape(b, t, FOX_HEADS, FOX_DIM),
            seq(lf8)[:, :, :FOX_HEADS], seq(bk_f), seq(bv_f), seq(ik_f), seq(ckv_f), seq(kpe_f))
    cache = (None,) * 8 if past is None else past

    lf_all = with_past(None if past is None else jnp.pad(cache[2], ((0, 0), (0, 0), (0, 2))), lf8, F32)
    cum_t = _cumsum_call(jnp.swapaxes(lf_all, 1, 2))
    ya = _fox_call(queries(fq), with_past(cache[0], fk_b, BF16),
                   jnp.swapaxes(with_past(cache[1], fv_b, BF16), 1, 2),
                   cum_t[:, :, pl_len:pl_len + t_att], jnp.swapaxes(cum_t, 1, 2), tq, pl_len)[:, :t]

    def rep(a, k):
        return jnp.tile(a.reshape(b, pl_len, -1), (1, 1, k))

    bias_t, far = _bias_tables(rel_bias, tq)
    ik_all = with_past(None if past is None else rep(cache[5], IDX_HEADS), ik_r, BF16)
    bk_all = with_past(None if past is None else rep(cache[3], DSA_HEADS), bk_r, BF16)
    bv_t = jnp.swapaxes(with_past(cache[4], bv_b, BF16), 1, 2)
    iw_t = jnp.swapaxes(queries(iw8), 1, 2)
    yb = _dsa_call(far, queries(iq), iw_t, ik_all, queries(bq), bk_all, bv_t, bias_t,
                   tq, pl_len, s_valid, k_sel)[:, :t]

    ckv_all = with_past(cache[6], ckv_f, F32).reshape(b * s_pad, MLA_KV_LORA)
    kpe_all = with_past(cache[7], kpe_f, BF16)
    kpe_all = jnp.pad(kpe_all, ((0, 0), (0, 0), (0, LANE - MLA_ROPE))).reshape(b * s_pad, LANE)
    kc, mv = _kvup_call(ckv_all, kpe_all, kvup_c, min(512, b * s_pad))
    yc = _mla_call(queries(qc), kc.reshape(b, s_pad, -1), jnp.swapaxes(mv.reshape(b, s_pad, -1), 1, 2),
                   tq, pl_len, s_valid)[:, :t]

    x1 = _merge_call(x2d, ya.reshape(n, -1), yb.reshape(n, -1), yc.reshape(n, -1), merge_c, tm)
    x2 = _ffn_call(x1, *ffn_c, tm)
    return x2.reshape(b, t, d), rows


def kernel(x_prompt, x_sample, cache_fox_k, cache_fox_v, cache_fox_logf, cache_dsa_k, cache_dsa_v, cache_idx_k, cache_mla_ckv, cache_mla_kpe, rel_bias, norm_mix, w_in, fox_gq, fox_gk, fox_bf, dsa_gq, dsa_gk, idx_gk, mla_gqa, mla_wqb, mla_gqn, mla_gqr, mla_gkv, mla_gkr, mla_wkvb, mla_gkn, w_fox_out, w_dsa_out, w_mla_out, w_gate, w_o, norm_ffn, w_ffn_in, w_ffn_out):
    caches = (cache_fox_k, cache_fox_v, cache_fox_logf, cache_dsa_k, cache_dsa_v,
              cache_idx_k, cache_mla_ckv, cache_mla_kpe)
    yp, ys = x_prompt, x_sample
    p_rows, s_rows = [], []
    for i in range(norm_mix.shape[0]):
        p = dict(norm_mix=norm_mix[i], w_in=w_in[i], fox_gq=fox_gq[i], fox_gk=fox_gk[i],
                 fox_bf=fox_bf[i], dsa_gq=dsa_gq[i], dsa_gk=dsa_gk[i], idx_gk=idx_gk[i],
                 mla_gqa=mla_gqa[i], mla_wqb=mla_wqb[i], mla_gqn=mla_gqn[i], mla_gqr=mla_gqr[i],
                 mla_gkv=mla_gkv[i], mla_gkr=mla_gkr[i], mla_wkvb=mla_wkvb[i], mla_gkn=mla_gkn[i],
                 w_fox_out=w_fox_out[i], w_dsa_out=w_dsa_out[i], w_mla_out=w_mla_out[i],
                 w_gate=w_gate[i], w_o=w_o[i], norm_ffn=norm_ffn[i], w_ffn_in=w_ffn_in[i],
                 w_ffn_out=w_ffn_out[i])
        yp, rows_p = _layer(yp, None, p, rel_bias)
        ys, rows_s = _layer(ys, tuple(c[i] for c in caches), p, rel_bias)
        p_rows.append(rows_p)
        s_rows.append(rows_s)

    def st(rows, j):
        return jnp.stack([r[j] for r in rows], axis=0)

    return ((yp, ys) + tuple(st(p_rows, j) for j in range(8)) + tuple(st(s_rows, j) for j in range(8)))
```

```python
import functools
import math

import numpy as np
import jax
import jax.numpy as jnp
from jax import lax
from jax.experimental import pallas as pl
from jax.experimental.pallas import tpu as pltpu

F32 = jnp.float32
BF16 = jnp.bfloat16

CHUNK = 64
EPS = 1e-6
FOX_HEADS, FOX_DIM = 6, 64
FOX_W = FOX_HEADS * FOX_DIM
DSA_HEADS, DSA_DIM = 4, 64
DSA_W = DSA_HEADS * DSA_DIM
IDX_HEADS, IDX_DIM = 8, 32
IDX_W = IDX_HEADS * IDX_DIM
TOPK_MAX = 256
MLA_HEADS = 6
MLA_Q_LORA, MLA_KV_LORA = 256, 128
MLA_NOPE, MLA_ROPE, MLA_V = 64, 32, 64
MLA_W = MLA_HEADS * MLA_V
ROPE_THETA = 10000.0
REL_BUCKETS, REL_MAX_DIST = 32, 128

LANE = 128
KB_SMALL, KB_BIG = 256, 512
BIAS_BACK = 768
BIAS_ROWS = BIAS_BACK + KB_BIG
LOG2E = math.log2(math.e)
M_INIT = -1e30
MASKED = -3e30
INT_MIN = -2 ** 31
KEY_NEG_INF = -2139095041
I16_MIN, I16_MAX = -2 ** 15, 2 ** 15 - 1
VMEM_LIMIT = 56 * 1024 * 1024

_C_FQ, _C_FK, _C_FV, _C_FG = 0, 384, 768, 1152
_C_BQ, _C_BK, _C_BV = 1280, 1536, 1792
_C_IQ, _C_IK, _C_IW = 1920, 2176, 2432
_C_QA, _C_KV, _C_PE = 2560, 2816, 2944
_C_END = 3072


def _dot(a, b):
    return jnp.dot(a, b, preferred_element_type=F32)


def _dot_nt(a, b):
    return lax.dot_general(a, b, (((1,), (1,)), ((), ())), preferred_element_type=F32)


def _dot_hilo(x, m):
    hi = x.astype(BF16)
    lo = (x - hi.astype(F32)).astype(BF16)
    return _dot(hi, m) + _dot(lo, m)


def _rms_rows(x, g):
    return x * lax.rsqrt(jnp.mean(x * x, axis=-1, keepdims=True) + EPS) * g


def _rms_groups(x, avg, g):
    return x * lax.rsqrt(_dot_hilo(x * x, avg) + EPS) * g


def _params(sem, vmem=VMEM_LIMIT):
    return pltpu.CompilerParams(dimension_semantics=sem, vmem_limit_bytes=vmem)


def _const_spec(shape):
    nd = len(shape)
    return pl.BlockSpec(shape, lambda *_: (0,) * nd)


def _proj_kernel(x_ref, gmix_ref, w_ref, bf_ref, gfq_ref, gfk_ref, gbq_ref, gbk_ref, gik_ref,
                 gqa_ref, gkv_ref, gkr_ref, wqb_ref, gqc_ref,
                 a64x6_ref, a64x4_ref, a32x8_ref, aqc_ref, ape_ref, rq_ref, rk_ref,
                 cosq_ref, sinq_ref, cosk_ref, sink_ref,
                 fq_o, fkf_o, fkb_o, fvf_o, fvb_o, lf_o, bq_o, bkf_o, bkr_o, bvf_o, bvb_o,
                 iq_o, ikf_o, ikr_o, iw_o, qc_o, ckv_o, kpe_o):
    x = x_ref[...]
    hb = _rms_rows(x, gmix_ref[...]).astype(BF16)

    def grp(lo, hi):
        return _dot(hb, w_ref[:, lo:hi])

    fq_o[...] = _rms_groups(grp(_C_FQ, _C_FK), a64x6_ref[...], gfq_ref[...]).astype(BF16)
    fk = _rms_groups(grp(_C_FK, _C_FV), a64x6_ref[...], gfk_ref[...])
    fkf_o[...] = fk
    fkb_o[...] = fk.astype(BF16)
    fv = grp(_C_FV, _C_FG)
    fvf_o[...] = fv
    fvb_o[...] = fv.astype(BF16)
    z = grp(_C_FG, _C_BQ) + bf_ref[...]
    lf = jnp.minimum(z, 0.0) - jnp.log1p(jnp.exp(-jnp.abs(z)))
    lf_o[...] = lf[:, :8]

    bq_o[...] = _rms_groups(grp(_C_BQ, _C_BK), a64x4_ref[...], gbq_ref[...]).astype(BF16)
    bk = _rms_groups(grp(_C_BK, _C_BV), a64x4_ref[...], gbk_ref[...])
    bkf_o[...] = bk[:, :DSA_DIM]
    bkr_o[...] = bk.astype(BF16)
    bv = grp(_C_BV, _C_IQ)[:, :DSA_DIM]
    bvf_o[...] = bv
    bvb_o[...] = bv.astype(BF16)
    iq_o[...] = grp(_C_IQ, _C_IK).astype(BF16)
    ik = _rms_groups(grp(_C_IK, _C_IW), a32x8_ref[...], gik_ref[...])
    ikf_o[...] = ik[:, :IDX_DIM]
    ikr_o[...] = ik.astype(BF16)
    iw_o[...] = (grp(_C_IW, _C_QA) * (1.0 / 16.0))[:, :8]

    cq = _rms_rows(grp(_C_QA, _C_KV), gqa_ref[...]).astype(BF16)
    cosq, sinq = cosq_ref[...], sinq_ref[...]
    for h in range(MLA_HEADS):
        sl = slice(h * LANE, (h + 1) * LANE)
        qh = _rms_groups(_dot(cq, wqb_ref[:, sl]), aqc_ref[...], gqc_ref[:, sl])
        qc_o[:, sl] = (qh * cosq + _dot_hilo(qh, rq_ref[...]) * sinq).astype(BF16)
    ckv_o[...] = _rms_rows(grp(_C_KV, _C_PE), gkv_ref[...])
    kp = _rms_groups(grp(_C_PE, _C_END), ape_ref[...], gkr_ref[...])
    kp = kp * cosk_ref[...] + _dot_hilo(kp, rk_ref[...]) * sink_ref[...]
    kpe_o[...] = kp[:, :MLA_ROPE]


def _proj_call(x2d, consts, tabs, tm, n_tab_blocks):
    n = x2d.shape[0]
    row = lambda w: pl.BlockSpec((tm, w), lambda i: (i, 0))
    tab = pl.BlockSpec((tm, LANE), lambda i: (i % n_tab_blocks, 0))
    in_specs = [row(x2d.shape[1])] + [_const_spec(c.shape) for c in consts] + [tab] * 4
    widths = [(FOX_W, BF16), (FOX_W, F32), (FOX_W, BF16), (FOX_W, F32), (FOX_W, BF16), (8, F32),
              (DSA_W, BF16), (DSA_DIM, F32), (DSA_W, BF16), (DSA_DIM, F32), (DSA_DIM, BF16),
              (IDX_W, BF16), (IDX_DIM, F32), (IDX_W, BF16), (8, F32),
              (MLA_HEADS * LANE, BF16), (MLA_KV_LORA, F32), (MLA_ROPE, F32)]
    return pl.pallas_call(
        _proj_kernel,
        grid=(n // tm,),
        in_specs=in_specs,
        out_specs=[row(w) for w, _ in widths],
        out_shape=[jax.ShapeDtypeStruct((n, w), d) for w, d in widths],
        compiler_params=_params(("parallel",)),
        name="proj",
    )(x2d, *consts, *tabs)


def _cumsum_kernel(x_ref, tri_ref, o_ref):
    nblk = x_ref.shape[2] // LANE

    def body(j, carry):
        s = pl.multiple_of(j * LANE, LANE)
        x = x_ref[0, :, pl.ds(s, LANE)]
        x1 = x.astype(BF16)
        r1 = x - x1.astype(F32)
        x2 = r1.astype(BF16)
        x3 = (r1 - x2.astype(F32)).astype(BF16)
        tri = tri_ref[...]
        out = _dot(x1, tri) + _dot(x2, tri) + _dot(x3, tri) + carry
        o_ref[0, :, pl.ds(s, LANE)] = out * LOG2E
        return out[:, LANE - 1:LANE]

    lax.fori_loop(0, nblk, body, jnp.zeros((8, 1), F32))


def _cumsum_call(lf_t):
    b, _, s = lf_t.shape
    tri = jnp.asarray(np.triu(np.ones((LANE, LANE), np.float32)), BF16)
    return pl.pallas_call(
        _cumsum_kernel,
        grid=(b,),
        in_specs=[pl.BlockSpec((1, 8, s), lambda i: (i, 0, 0)), _const_spec((LANE, LANE))],
        out_specs=pl.BlockSpec((1, 8, s), lambda i: (i, 0, 0)),
        out_shape=jax.ShapeDtypeStruct((b, 8, s), F32),
        compiler_params=_params(("parallel",)),
        name="cumsum",
    )(lf_t, tri)


def _softmax_steps(states, logits, vts):
    mid = []
    for (m, l, acc), s in zip(states, logits):
        m_new = jnp.maximum(m, jnp.max(s, axis=0, keepdims=True))
        alpha = jnp.exp2(m - m_new)
        p = jnp.exp2(s - m_new)
        mid.append((m_new, alpha * l + jnp.sum(p, axis=0, keepdims=True), alpha * acc, p.astype(BF16)))
    return tuple((m_new, l, acc + _dot(vt, p)) for (m_new, l, acc, p), vt in zip(mid, vts))


def _init_state(n_heads, tq):
    return tuple((jnp.full((1, tq), M_INIT, F32), jnp.zeros((1, tq), F32), jnp.zeros((64, tq), F32))
                 for _ in range(n_heads))


def _finish(state, o_ref):
    out_t = jnp.concatenate([acc / l for _, l, acc in state], axis=0)
    o_ref[0] = out_t.T.astype(o_ref.dtype)


def _sweep(n_small, step, carry):
    ratio = KB_BIG // KB_SMALL
    n_big = n_small // ratio

    def big(j, c):
        return step(pl.multiple_of(j * KB_BIG, KB_BIG), KB_BIG, c)

    def small(j, c):
        return step(pl.multiple_of(j * KB_SMALL, KB_SMALL), KB_SMALL, c)

    carry = lax.fori_loop(0, n_big, big, carry)
    return lax.fori_loop(n_big * ratio, n_small, small, carry)


def _fox_kernel(q_ref, k_ref, vt_ref, cq_ref, ck_ref, o_ref, *, tq, past):
    q0 = past + pl.program_id(1) * tq
    lane = lax.broadcasted_iota(jnp.int32, (tq, LANE), 1)
    q_heads, cq = [], []
    for h in range(FOX_HEADS):
        qp = q_ref[0, :, (h // 2) * LANE:(h // 2 + 1) * LANE]
        q_heads.append(jnp.where((lane < 64) == (h % 2 == 0), qp, jnp.zeros_like(qp)))
        cq.append(cq_ref[0, h:h + 1, :])

    def step(ks, width, state, masked=False):
        if masked:
            ok = (ks + lax.broadcasted_iota(jnp.int32, (width, tq), 0)
                  <= q0 + lax.broadcasted_iota(jnp.int32, (width, tq), 1))
        logits = []
        for h in range(FOX_HEADS):
            kb = k_ref[0, pl.ds(ks, width), (h // 2) * LANE:(h // 2 + 1) * LANE]
            s = _dot_nt(kb, q_heads[h]) + cq[h] - ck_ref[0, pl.ds(ks, width), h:h + 1]
            logits.append(jnp.where(ok, s, MASKED) if masked else s)
        vts = [vt_ref[0, h * 64:(h + 1) * 64, pl.ds(ks, width)] for h in range(FOX_HEADS)]
        return _softmax_steps(state, logits, vts)

    n_free = q0 // KB_SMALL
    state = _sweep(n_free, step, _init_state(FOX_HEADS, tq))
    state = step(pl.multiple_of(n_free * KB_SMALL, KB_SMALL), KB_SMALL, state, masked=True)
    _finish(state, o_ref)


def _fox_call(q, k, vt, cq, ck, tq, past):
    b, t, _ = q.shape
    s = k.shape[1]
    return pl.pallas_call(
        functools.partial(_fox_kernel, tq=tq, past=past),
        grid=(b, t // tq),
        in_specs=[
            pl.BlockSpec((1, tq, FOX_W), lambda i, j: (i, j, 0)),
            pl.BlockSpec((1, s, FOX_W), lambda i, j: (i, 0, 0)),
            pl.BlockSpec((1, FOX_W, s), lambda i, j: (i, 0, 0)),
            pl.BlockSpec((1, 8, tq), lambda i, j: (i, 0, j)),
            pl.BlockSpec((1, s, 8), lambda i, j: (i, 0, 0)),
        ],
        out_specs=pl.BlockSpec((1, tq, FOX_W), lambda i, j: (i, j, 0)),
        out_shape=jax.ShapeDtypeStruct((b, t, FOX_W), BF16),
        compiler_params=_params(("parallel", "parallel")),
        name="fox",
    )(q, k, vt, cq, ck)


def _mla_kernel(q_ref, k_ref, vt_ref, o_ref, *, tq, past, s_valid):
    q0 = past + pl.program_id(1) * tq
    q_heads = [q_ref[0, :, h * LANE:(h + 1) * LANE] for h in range(MLA_HEADS)]

    def step(ks, width, state, masked=False):
        if masked:
            kpos = ks + lax.broadcasted_iota(jnp.int32, (width, tq), 0)
            qchunk = (q0 + lax.broadcasted_iota(jnp.int32, (width, tq), 1)) // CHUNK
            ok = (kpos // CHUNK <= qchunk) & (kpos < s_valid)
        logits = []
        for h in range(MLA_HEADS):
            s = _dot_nt(k_ref[0, pl.ds(ks, width), h * LANE:(h + 1) * LANE], q_heads[h])
            logits.append(jnp.where(ok, s, MASKED) if masked else s)
        vts = [vt_ref[0, h * 64:(h + 1) * 64, pl.ds(ks, width)] for h in range(MLA_HEADS)]
        return _softmax_steps(state, logits, vts)

    n_free = q0 // KB_SMALL
    state = _sweep(n_free, step, _init_state(MLA_HEADS, tq))
    state = step(pl.multiple_of(n_free * KB_SMALL, KB_SMALL), KB_SMALL, state, masked=True)
    _finish(state, o_ref)


def _mla_call(q, k, vt, tq, past, s_valid):
    b, t, _ = q.shape
    s = k.shape[1]
    return pl.pallas_call(
        functools.partial(_mla_kernel, tq=tq, past=past, s_valid=s_valid),
        grid=(b, t // tq),
        in_specs=[
            pl.BlockSpec((1, tq, MLA_HEADS * LANE), lambda i, j: (i, j, 0)),
            pl.BlockSpec((1, s, MLA_HEADS * LANE), lambda i, j: (i, 0, 0)),
            pl.BlockSpec((1, MLA_W, s), lambda i, j: (i, 0, 0)),
        ],
        out_specs=pl.BlockSpec((1, tq, MLA_W), lambda i, j: (i, j, 0)),
        out_shape=jax.ShapeDtypeStruct((b, t, MLA_W), BF16),
        compiler_params=_params(("parallel", "parallel")),
        name="mla",
    )(q, k, vt)


def _kvup_kernel(ckv_ref, kpe_ref, wk_ref, wv_ref, gkn_ref, akn_ref, place_ref, k_o, v_o):
    cb = ckv_ref[...].astype(BF16)
    v_o[...] = _dot(cb, wv_ref[...]).astype(BF16)
    pe = _dot(kpe_ref[...], place_ref[...])
    for h in range(MLA_HEADS):
        sl = slice(h * LANE, (h + 1) * LANE)
        kn = _rms_groups(_dot(cb, wk_ref[:, sl]), akn_ref[...], gkn_ref[...])
        k_o[:, sl] = (kn + pe).astype(BF16)


def _kvup_call(ckv2d, kpe2d, consts, tm):
    n = ckv2d.shape[0]
    row = lambda w: pl.BlockSpec((tm, w), lambda i: (i, 0))
    return pl.pallas_call(
        _kvup_kernel,
        grid=(n // tm,),
        in_specs=[row(MLA_KV_LORA), row(LANE)] + [_const_spec(c.shape) for c in consts],
        out_specs=[row(MLA_HEADS * LANE), row(MLA_W)],
        out_shape=[jax.ShapeDtypeStruct((n, MLA_HEADS * LANE), BF16),
                   jax.ShapeDtypeStruct((n, MLA_W), BF16)],
        compiler_params=_params(("parallel",)),
        name="kvup",
    )(ckv2d, kpe2d, *consts)


def _dsa_kernel(far_ref, iq_ref, iw_ref, ik_ref, bq_ref, bk_ref, bvt_ref, bias_ref, low_ref,
                o_ref, keys_ref, half_ref, *, tq, past, s_valid, k_sel):
    kb = KB_BIG
    q0 = past + pl.program_id(1) * tq
    nkb = (jnp.minimum(q0 + tq, s_valid) + kb - 1) // kb
    n_far = jnp.maximum(nkb - 2, 0)
    qpos = q0 + lax.broadcasted_iota(jnp.int32, (kb, tq), 1)
    koff = lax.broadcasted_iota(jnp.int32, (kb, tq), 0)
    qlane = lax.broadcasted_iota(jnp.int32, (tq, IDX_W), 1)

    iq = iq_ref[0]
    iq_heads = [jnp.where(qlane // IDX_DIM == h, iq, jnp.zeros_like(iq)) for h in range(IDX_HEADS)]

    def score_body(j, _, masked):
        ks = pl.multiple_of(j * kb, kb)
        ikb = ik_ref[0, pl.ds(ks, kb), :]
        acc = jnp.zeros((kb, tq), F32)
        for h in range(IDX_HEADS):
            acc = acc + jnp.maximum(_dot_nt(ikb, iq_heads[h]), 0.0) * iw_ref[0, h:h + 1, :]
        bits = pltpu.bitcast(acc, jnp.int32)
        key = jnp.where(bits < 0, bits ^ 0x7FFFFFFF, bits)
        if masked:
            kpos = ks + koff
            key = jnp.where(kpos // CHUNK <= qpos // CHUNK, jnp.where(kpos < s_valid, key, KEY_NEG_INF),
                            KEY_NEG_INF)
        keys_ref[pl.ds(ks, kb), :] = key
        half_ref[pl.ds(ks, kb), :] = (key >> 16).astype(jnp.int16)
        return 0

    lax.fori_loop(0, nkb - 1, functools.partial(score_body, masked=False), 0)
    score_body(nkb - 1, 0, masked=True)

    def count_ge16(cand):
        cand = cand.astype(jnp.int16)

        def body(j, c16):
            ks = pl.multiple_of(j * kb, kb)
            hit = jnp.where(half_ref[pl.ds(ks, kb), :] >= cand, jnp.int16(1), jnp.int16(0))
            parts = [hit[i * 16:(i + 1) * 16] for i in range(kb // 16)]
            while len(parts) > 1:
                parts = [a + b for a, b in zip(parts[::2], parts[1::2])]
            return c16 + parts[0]
        c16 = lax.fori_loop(0, nkb, body, jnp.zeros((16, tq), jnp.int16))
        return jnp.sum(c16.astype(jnp.int32), axis=0, keepdims=True)

    def count_gt16(t):
        return jnp.where(t < I16_MAX, count_ge16(jnp.minimum(t + 1, I16_MAX)), 0)

    def kth_largest16(rank):
        def body(i, ub):
            c = ub | jnp.left_shift(jnp.int32(1), 15 - i)
            return jnp.where(count_ge16(c + I16_MIN) >= rank, c, ub)
        return lax.fori_loop(0, 16, body, jnp.zeros((1, tq), jnp.int32)) + I16_MIN

    t_hi = kth_largest16(k_sel)
    above = count_gt16(t_hi)

    def low_half_body(j, _):
        ks = pl.multiple_of(j * kb, kb)
        key = keys_ref[pl.ds(ks, kb), :]
        low = jnp.where((key >> 16) == t_hi, (key & 0xFFFF) + I16_MIN, I16_MIN)
        half_ref[pl.ds(ks, kb), :] = low.astype(jnp.int16)
        return 0

    lax.fori_loop(0, nkb, low_half_body, 0)
    t_lo = kth_largest16(k_sel - above)
    thr = t_hi * 65536 + (t_lo - I16_MIN)
    need = (k_sel - above - count_gt16(t_lo)).astype(F32)
    thr_next = thr + 1

    bq = bq_ref[0]
    blane = lax.broadcasted_iota(jnp.int32, (tq, DSA_W), 1)
    bq_heads = [jnp.where(blane // DSA_DIM == h, bq, jnp.zeros_like(bq)) for h in range(DSA_HEADS)]

    def att_body(j, carry, near):
        seen, state = carry
        ks = pl.multiple_of(j * kb, kb)
        key = keys_ref[pl.ds(ks, kb), :]
        tie = jnp.where(key == thr, 1.0, 0.0)
        rank = seen + _dot(low_ref[...], tie.astype(BF16))
        sel = key >= jnp.maximum(jnp.where(rank < need, thr, thr_next), KEY_NEG_INF + 1)
        bkb = bk_ref[0, pl.ds(ks, kb), :]
        vt = bvt_ref[0, :, pl.ds(ks, kb)]
        if near:
            bs = pl.multiple_of(ks - (q0 - BIAS_BACK), KB_SMALL)
        logits = []
        for h in range(DSA_HEADS):
            s = _dot_nt(bkb, bq_heads[h])
            s = s + (bias_ref[h, pl.ds(bs, kb), :] if near else far_ref[h])
            logits.append(jnp.where(sel, s, MASKED))
        return (seen + jnp.sum(tie, axis=0, keepdims=True),
                _softmax_steps(state, logits, [vt] * DSA_HEADS))

    carry = (jnp.zeros((1, tq), F32), _init_state(DSA_HEADS, tq))
    carry = lax.fori_loop(0, n_far, functools.partial(att_body, near=False), carry)
    carry = lax.fori_loop(n_far, nkb, functools.partial(att_body, near=True), carry)
    _finish(carry[1], o_ref)


def _dsa_call(far, iq, iw_t, ik, bq, bk, bv_t, bias_t, tq, past, s_valid, k_sel):
    b, t, _ = iq.shape
    s = ik.shape[1]
    low = jnp.asarray(np.tril(np.ones((KB_BIG, KB_BIG), np.float32), -1), BF16)
    return pl.pallas_call(
        functools.partial(_dsa_kernel, tq=tq, past=past, s_valid=s_valid, k_sel=k_sel),
        grid=(b, t // tq),
        in_specs=[
            pl.BlockSpec(memory_space=pltpu.SMEM),
            pl.BlockSpec((1, tq, IDX_W), lambda i, j: (i, j, 0)),
            pl.BlockSpec((1, 8, tq), lambda i, j: (i, 0, j)),
            pl.BlockSpec((1, s, IDX_W), lambda i, j: (i, 0, 0)),
            pl.BlockSpec((1, tq, DSA_W), lambda i, j: (i, j, 0)),
            pl.BlockSpec((1, s, DSA_W), lambda i, j: (i, 0, 0)),
            pl.BlockSpec((1, DSA_DIM, s), lambda i, j: (i, 0, 0)),
            _const_spec(bias_t.shape),
            _const_spec(low.shape),
        ],
        out_specs=pl.BlockSpec((1, tq, DSA_W), lambda i, j: (i, j, 0)),
        out_shape=jax.ShapeDtypeStruct((b, t, DSA_W), BF16),
        scratch_shapes=[pltpu.VMEM((s, tq), jnp.int32), pltpu.VMEM((s, tq), jnp.int16)],
        compiler_params=_params(("parallel", "parallel")),
        name="dsa",
    )(far, iq, iw_t, ik, bq, bk, bv_t, bias_t, low)


def _merge_kernel(x_ref, ya_ref, yb_ref, yc_ref, g_ref, wg_ref, wa_ref, wb_ref, wc_ref, wo_ref, o_ref):
    x = x_ref[...]
    d = x.shape[1]
    hb = _rms_rows(x, g_ref[...]).astype(BF16)
    mix = jnp.zeros_like(x)
    for i, (y_ref, w_ref) in enumerate(((ya_ref, wa_ref), (yb_ref, wb_ref), (yc_ref, wc_ref))):
        gate = jax.nn.sigmoid(_dot(hb, wg_ref[:, i * d:(i + 1) * d]))
        mix = mix + gate * _dot(y_ref[...], w_ref[...])
    o_ref[...] = x + _dot(mix.astype(BF16), wo_ref[...])


def _merge_call(x2d, ya, yb, yc, consts, tm):
    n, d = x2d.shape
    row = lambda w: pl.BlockSpec((tm, w), lambda i: (i, 0))
    return pl.pallas_call(
        _merge_kernel,
        grid=(n // tm,),
        in_specs=[row(d), row(FOX_W), row(DSA_W), row(MLA_W)] + [_const_spec(c.shape) for c in consts],
        out_specs=row(d),
        out_shape=jax.ShapeDtypeStruct((n, d), F32),
        compiler_params=_params(("parallel",)),
        name="merge",
    )(x2d, ya, yb, yc, *consts)


def _ffn_kernel(x_ref, g_ref, wi_ref, wo_ref, o_ref, *, n_chunks):
    x = x_ref[...]
    hb = _rms_rows(x, g_ref[...]).astype(BF16)
    out = x
    for c in range(n_chunks):
        gu = _dot(hb, wi_ref[c])
        half = gu.shape[1] // 2
        gt, up = gu[:, :half], gu[:, half:]
        out = out + _dot((gt * jax.nn.sigmoid(gt) * up).astype(BF16), wo_ref[c])
    o_ref[...] = out


def _ffn_call(x2d, g, wi, wo, tm):
    n, d = x2d.shape
    row = pl.BlockSpec((tm, d), lambda i: (i, 0))
    return pl.pallas_call(
        functools.partial(_ffn_kernel, n_chunks=wi.shape[0]),
        grid=(n // tm,),
        in_specs=[row, _const_spec(g.shape), _const_spec(wi.shape), _const_spec(wo.shape)],
        out_specs=row,
        out_shape=jax.ShapeDtypeStruct((n, d), F32),
        compiler_params=_params(("parallel",)),
        name="ffn",
    )(x2d, g, wi, wo)


def _block_avg(blocks, width):
    m = np.zeros((width, width), np.float32)
    for lo, hi in blocks:
        m[lo:hi, lo:hi] = 1.0 / (hi - lo)
    return jnp.asarray(m, BF16)


def _rot_matrix(base):
    r = np.zeros((LANE, LANE), np.float32)
    half = MLA_ROPE // 2
    for i in range(half):
        r[base + half + i, base + i] = -1.0
        r[base + i, base + half + i] = 1.0
    return jnp.asarray(r, BF16)


def _rope_tables(pos, base):
    half = MLA_ROPE // 2
    freq = ROPE_THETA ** (-jnp.arange(half, dtype=F32) / half)
    ang = pos.astype(F32)[:, None] * freq[None, :]
    cos, sin = jnp.cos(ang), jnp.sin(ang)
    n = pos.shape[0]
    ct = jnp.ones((n, LANE), F32).at[:, base:base + MLA_ROPE].set(jnp.concatenate([cos, cos], 1))
    st = jnp.zeros((n, LANE), F32).at[:, base:base + MLA_ROPE].set(jnp.concatenate([sin, sin], 1))
    return ct, st


def _t5_bucket(rel):
    nb = REL_BUCKETS // 2
    max_exact = nb // 2
    side = jnp.where(rel > 0, nb, 0)
    n = jnp.abs(rel)
    large = max_exact + (jnp.log(jnp.maximum(n, 1).astype(F32) / max_exact)
                         / math.log(REL_MAX_DIST / max_exact) * (nb - max_exact)).astype(jnp.int32)
    large = jnp.minimum(large, nb - 1)
    return side + jnp.where(n < max_exact, n, large)


def _bias_tables(rel_bias, tq):
    c = jnp.arange(BIAS_ROWS, dtype=jnp.int32)[:, None]
    r = jnp.arange(tq, dtype=jnp.int32)[None, :]
    onehot = (_t5_bucket(c - BIAS_BACK - r)[None] == jnp.arange(REL_BUCKETS)[:, None, None]).astype(F32)
    near = jnp.einsum('bcr,bh->hcr', onehot, rel_bias.astype(F32), precision=lax.Precision.HIGHEST) * LOG2E
    far = rel_bias[_t5_bucket(jnp.int32(-REL_MAX_DIST))].astype(F32) * LOG2E
    return near, far


def _pad_cols(w, width):
    return jnp.pad(w, ((0, 0), (0, width - w.shape[1])))


def _tile_vec(g, reps, scale=1.0):
    return (jnp.tile(g.astype(F32), reps) * scale)[None, :]


def _layer_consts(p):
    splits = np.cumsum([FOX_W, FOX_W, FOX_W, FOX_HEADS, DSA_W, DSA_DIM, DSA_DIM, IDX_W, IDX_DIM, IDX_HEADS,
                        MLA_Q_LORA, MLA_KV_LORA, MLA_ROPE])[:-1]
    fq, fk, fv, fg, bq, bk, bv, iq, ik, iw, cqa, ckva, ckpe = jnp.split(p['w_in'], [int(v) for v in splits], axis=1)
    w_all = jnp.concatenate([
        fq, fk, fv, _pad_cols(fg, LANE), bq, jnp.tile(bk, (1, DSA_HEADS)), _pad_cols(bv, LANE),
        iq, jnp.tile(ik, (1, IDX_HEADS)), _pad_cols(iw, LANE), cqa, ckva, _pad_cols(ckpe, LANE)],
        axis=1).astype(BF16)
    assert w_all.shape[1] == _C_END
    d_qk = MLA_NOPE + MLA_ROPE
    wqb = p['mla_wqb'].reshape(MLA_Q_LORA, MLA_HEADS, d_qk)
    wqb = jnp.pad(wqb, ((0, 0), (0, 0), (0, LANE - d_qk))).reshape(MLA_Q_LORA, MLA_HEADS * LANE).astype(BF16)
    gqc = jnp.concatenate([p['mla_gqn'], p['mla_gqr'], jnp.zeros((LANE - d_qk,), F32)]).astype(F32)
    proj = [
        p['norm_mix'].astype(F32)[None, :], w_all,
        _pad_cols(p['fox_bf'].astype(F32)[None, :], LANE),
        _tile_vec(p['fox_gq'], FOX_HEADS, FOX_DIM ** -0.5 * LOG2E), _tile_vec(p['fox_gk'], FOX_HEADS),
        _tile_vec(p['dsa_gq'], DSA_HEADS, DSA_DIM ** -0.5 * LOG2E), _tile_vec(p['dsa_gk'], DSA_HEADS),
        _tile_vec(p['idx_gk'], IDX_HEADS),
        p['mla_gqa'].astype(F32)[None, :], p['mla_gkv'].astype(F32)[None, :],
        _pad_cols(p['mla_gkr'].astype(F32)[None, :], LANE),
        wqb, _tile_vec(gqc, MLA_HEADS, d_qk ** -0.5 * LOG2E),
        _block_avg([(i * 64, i * 64 + 64) for i in range(6)], FOX_W),
        _block_avg([(i * 64, i * 64 + 64) for i in range(4)], DSA_W),
        _block_avg([(i * 32, i * 32 + 32) for i in range(8)], IDX_W),
        _block_avg([(0, MLA_NOPE), (MLA_NOPE, d_qk)], LANE),
        _block_avg([(0, MLA_ROPE)], LANE),
        _rot_matrix(MLA_NOPE), _rot_matrix(0),
    ]
    wkvb = p['mla_wkvb'].reshape(MLA_KV_LORA, MLA_HEADS, MLA_NOPE + MLA_V)
    wk = jnp.pad(wkvb[:, :, :MLA_NOPE], ((0, 0), (0, 0), (0, LANE - MLA_NOPE)))
    wk = wk.reshape(MLA_KV_LORA, MLA_HEADS * LANE).astype(BF16)
    wv = wkvb[:, :, MLA_NOPE:].reshape(MLA_KV_LORA, MLA_W).astype(BF16)
    place = np.zeros((LANE, LANE), np.float32)
    place[np.arange(MLA_ROPE), MLA_NOPE + np.arange(MLA_ROPE)] = 1.0
    kvup = [wk, wv, _pad_cols(p['mla_gkn'].astype(F32)[None, :], LANE),
            _block_avg([(0, MLA_NOPE)], LANE), jnp.asarray(place, BF16)]
    merge = [p['norm_mix'].astype(F32)[None, :], p['w_gate'].astype(BF16), p['w_fox_out'].astype(BF16),
             p['w_dsa_out'].astype(BF16), p['w_mla_out'].astype(BF16), p['w_o'].astype(BF16)]
    d_ff = p['w_ffn_out'].shape[0]
    n_chunks = 2
    ck = d_ff // n_chunks
    wi = p['w_ffn_in']
    wi = jnp.stack([jnp.concatenate([wi[:, c * ck:(c + 1) * ck], wi[:, d_ff + c * ck:d_ff + (c + 1) * ck]], 1)
                    for c in range(n_chunks)]).astype(BF16)
    wo = p['w_ffn_out'].reshape(n_chunks, ck, -1).astype(BF16)
    ffn = [p['norm_ffn'].astype(F32)[None, :], wi, wo]
    return proj, kvup, merge, ffn


def _round_up(v, m):
    return -(-v // m) * m


def _layer(x, past, p, rel_bias, bias_cache):
    b, t, d = x.shape
    pl_len = 0 if past is None else past[0].shape[1]
    s_valid = pl_len + t
    n = b * t
    tm = min(512, n)
    t_att = _round_up(t, LANE)
    tq = min(256, t_att)
    assert pl_len % KB_SMALL == 0 and t_att % tq == 0
    s_pad = _round_up(pl_len + t_att - tq + max(tq, KB_SMALL), KB_BIG)
    k_sel = min(TOPK_MAX, s_valid // 4)
    proj_c, kvup_c, merge_c, ffn_c = _layer_consts(p)

    pos = pl_len + jnp.arange(t, dtype=jnp.int32)
    tabs = _rope_tables(pos, MLA_NOPE) + _rope_tables(pos, 0)
    if t < tm:
        tabs = tuple(jnp.tile(a, (tm // t, 1)) for a in tabs)
    n_tab_blocks = max(t // tm, 1)

    x2d = x.reshape(n, d)
    (fq, fk_f, fk_b, fv_f, fv_b, lf8, bq, bk_f, bk_r, bv_f, bv_b, iq, ik_f, ik_r, iw8, qc, ckv_f, kpe_f
     ) = _proj_call(x2d, proj_c, tabs, tm, n_tab_blocks)

    def seq(a):
        return a.reshape(b, t, a.shape[-1])

    def queries(a):
        return jnp.pad(seq(a), ((0, 0), (0, t_att - t), (0, 0)))

    def with_past(cached, new, dtype):
        new = seq(new).astype(dtype)
        if past is not None:
            new = jnp.concatenate([cached.reshape(b, pl_len, -1).astype(dtype), new], axis=1)
        return jnp.pad(new, ((0, 0), (0, s_pad - s_valid), (0, 0)))

    rows = (seq(fk_f).reshape(b, t, FOX_HEADS, FOX_DIM), seq(fv_f).reshape(b, t, FOX_HEADS, FOX_DIM),
            seq(lf8)[:, :, :FOX_HEADS], seq(bk_f), seq(bv_f), seq(ik_f), seq(ckv_f), seq(kpe_f))
    cache = (None,) * 8 if past is None else past

    lf_all = with_past(None if past is None else jnp.pad(cache[2], ((0, 0), (0, 0), (0, 2))), lf8, F32)
    cum_t = _cumsum_call(jnp.swapaxes(lf_all, 1, 2))
    ya = _fox_call(queries(fq), with_past(cache[0], fk_b, BF16),
                   jnp.swapaxes(with_past(cache[1], fv_b, BF16), 1, 2),
                   cum_t[:, :, pl_len:pl_len + t_att], jnp.swapaxes(cum_t, 1, 2), tq, pl_len)[:, :t]

    def rep(a, k):
        return jnp.tile(a.reshape(b, pl_len, -1), (1, 1, k))

    if tq not in bias_cache:
        bias_cache[tq] = _bias_tables(rel_bias, tq)
    bias_t, far = bias_cache[tq]
    ik_all = with_past(None if past is None else rep(cache[5], IDX_HEADS), ik_r, BF16)
    bk_all = with_past(None if past is None else rep(cache[3], DSA_HEADS), bk_r, BF16)
    bv_t = jnp.swapaxes(with_past(cache[4], bv_b, BF16), 1, 2)
    iw_t = jnp.swapaxes(queries(iw8), 1, 2)
    yb = _dsa_call(far, queries(iq), iw_t, ik_all, queries(bq), bk_all, bv_t, bias_t,
                   tq, pl_len, s_valid, k_sel)[:, :t]

    ckv_all = with_past(cache[6], ckv_f, F32).reshape(b * s_pad, MLA_KV_LORA)
    kpe_all = with_past(cache[7], kpe_f, BF16)
    kpe_all = jnp.pad(kpe_all, ((0, 0), (0, 0), (0, LANE - MLA_ROPE))).reshape(b * s_pad, LANE)
    kc, mv = _kvup_call(ckv_all, kpe_all, kvup_c, min(512, b * s_pad))
    yc = _mla_call(queries(qc), kc.reshape(b, s_pad, -1), jnp.swapaxes(mv.reshape(b, s_pad, -1), 1, 2),
                   tq, pl_len, s_valid)[:, :t]

    x1 = _merge_call(x2d, ya.reshape(n, -1), yb.reshape(n, -1), yc.reshape(n, -1), merge_c, tm)
    x2 = _ffn_call(x1, *ffn_c, tm)
    return x2.reshape(b, t, d), rows


def kernel(x_prompt, x_sample, cache_fox_k, cache_fox_v, cache_fox_logf, cache_dsa_k, cache_dsa_v, cache_idx_k, cache_mla_ckv, cache_mla_kpe, rel_bias, norm_mix, w_in, fox_gq, fox_gk, fox_bf, dsa_gq, dsa_gk, idx_gk, mla_gqa, mla_wqb, mla_gqn, mla_gqr, mla_gkv, mla_gkr, mla_wkvb, mla_gkn, w_fox_out, w_dsa_out, w_mla_out, w_gate, w_o, norm_ffn, w_ffn_in, w_ffn_out):
    caches = (cache_fox_k, cache_fox_v, cache_fox_logf, cache_dsa_k, cache_dsa_v,
              cache_idx_k, cache_mla_ckv, cache_mla_kpe)
    yp, ys = x_prompt, x_sample
    p_rows, s_rows, bias_cache = [], [], {}
    for i in range(norm_mix.shape[0]):
        p = dict(norm_mix=norm_mix[i], w_in=w_in[i], fox_gq=fox_gq[i], fox_gk=fox_gk[i],
                 fox_bf=fox_bf[i], dsa_gq=dsa_gq[i], dsa_gk=dsa_gk[i], idx_gk=idx_gk[i],
                 mla_gqa=mla_gqa[i], mla_wqb=mla_wqb[i], mla_gqn=mla_gqn[i], mla_gqr=mla_gqr[i],
                 mla_gkv=mla_gkv[i], mla_gkr=mla_gkr[i], mla_wkvb=mla_wkvb[i], mla_gkn=mla_gkn[i],
                 w_fox_out=w_fox_out[i], w_dsa_out=w_dsa_out[i], w_mla_out=w_mla_out[i],
                 w_gate=w_gate[i], w_o=w_o[i], norm_ffn=norm_ffn[i], w_ffn_in=w_ffn_in[i],
                 w_ffn_out=w_ffn_out[i])
        yp, rows_p = _layer(yp, None, p, rel_bias, bias_cache)
        ys, rows_s = _layer(ys, tuple(c[i] for c in caches), p, rel_bias, bias_cache)
        p_rows.append(rows_p)
        s_rows.append(rows_s)

    def st(rows, j):
        return jnp.stack([r[j] for r in rows], axis=0)

    return ((yp, ys) + tuple(st(p_rows, j) for j in range(8)) + tuple(st(s_rows, j) for j in range(8)))
```

```python
import functools
import math

import numpy as np
import jax
import jax.numpy as jnp
from jax import lax
from jax.experimental import pallas as pl
from jax.experimental.pallas import tpu as pltpu

F32 = jnp.float32
BF16 = jnp.bfloat16

CHUNK = 64
EPS = 1e-6
FOX_HEADS, FOX_DIM = 6, 64
FOX_W = FOX_HEADS * FOX_DIM
DSA_HEADS, DSA_DIM = 4, 64
DSA_W = DSA_HEADS * DSA_DIM
IDX_HEADS, IDX_DIM = 8, 32
IDX_W = IDX_HEADS * IDX_DIM
TOPK_MAX = 256
MLA_HEADS = 6
MLA_Q_LORA, MLA_KV_LORA = 256, 128
MLA_NOPE, MLA_ROPE, MLA_V = 64, 32, 64
MLA_W = MLA_HEADS * MLA_V
ROPE_THETA = 10000.0
REL_BUCKETS, REL_MAX_DIST = 32, 128

LANE = 128
KB_SMALL, KB_BIG = 256, 512
V_ROWS = 80
BIAS_BACK = 768
BIAS_ROWS = BIAS_BACK + KB_BIG
LOG2E = math.log2(math.e)
M_INIT = -1e30
MASKED = -3e30
INT_MIN = -2 ** 31
KEY_NEG_INF = -2139095041
I16_MIN, I16_MAX = -2 ** 15, 2 ** 15 - 1
VMEM_LIMIT = 56 * 1024 * 1024

_C_FQ, _C_FK, _C_FV, _C_FG = 0, 384, 768, 1152
_C_BQ, _C_BK, _C_BV = 1280, 1536, 1792
_C_IQ, _C_IK, _C_IW = 1920, 2176, 2432
_C_QA, _C_KV, _C_PE = 2560, 2816, 2944
_C_END = 3072


def _dot(a, b):
    return jnp.dot(a, b, preferred_element_type=F32)


def _dot_nt(a, b):
    return lax.dot_general(a, b, (((1,), (1,)), ((), ())), preferred_element_type=F32)


def _dot_hilo(x, m):
    hi = x.astype(BF16)
    lo = (x - hi.astype(F32)).astype(BF16)
    return _dot(hi, m) + _dot(lo, m)


def _rms_rows(x, g):
    return x * lax.rsqrt(jnp.mean(x * x, axis=-1, keepdims=True) + EPS) * g


def _rms_groups(x, avg, g):
    return x * lax.rsqrt(_dot((x * x).astype(BF16), avg) + EPS) * g


def _params(sem, vmem=VMEM_LIMIT):
    return pltpu.CompilerParams(dimension_semantics=sem, vmem_limit_bytes=vmem)


def _const_spec(shape):
    nd = len(shape)
    return pl.BlockSpec(shape, lambda *_: (0,) * nd)


def _proj_kernel(x_ref, gmix_ref, w_ref, bf_ref, gfq_ref, gfk_ref, gbq_ref, gbk_ref, gik_ref,
                 gqa_ref, gkv_ref, gkr_ref, wqb_ref, gqc_ref,
                 a64x6_ref, a64x4_ref, a32x8_ref, aqc_ref, ape_ref, rq_ref, rk_ref,
                 cosq_ref, sinq_ref, cosk_ref, sink_ref,
                 fq_o, fkf_o, fkb_o, fvf_o, fvb_o, lf_o, bq_o, bkf_o, bkr_o, bvf_o, bvb_o,
                 iq_o, ikf_o, ikr_o, iw_o, qc_o, ckv_o, kpe_o):
    x = x_ref[...]
    hb = _rms_rows(x, gmix_ref[...]).astype(BF16)

    def grp(lo, hi):
        return _dot(hb, w_ref[:, lo:hi])

    fq_o[...] = _rms_groups(grp(_C_FQ, _C_FK), a64x6_ref[...], gfq_ref[...]).astype(BF16)
    fk = _rms_groups(grp(_C_FK, _C_FV), a64x6_ref[...], gfk_ref[...])
    fkf_o[...] = fk
    fkb_o[...] = fk.astype(BF16)
    fv = grp(_C_FV, _C_FG)
    fvf_o[...] = fv
    fvb_o[...] = fv.astype(BF16)
    z = grp(_C_FG, _C_BQ) + bf_ref[...]
    lf = jnp.minimum(z, 0.0) - jnp.log1p(jnp.exp(-jnp.abs(z)))
    lf_o[...] = lf[:, :8]

    bq_o[...] = _rms_groups(grp(_C_BQ, _C_BK), a64x4_ref[...], gbq_ref[...]).astype(BF16)
    bk = _rms_groups(grp(_C_BK, _C_BV), a64x4_ref[...], gbk_ref[...])
    bkf_o[...] = bk[:, :DSA_DIM]
    bkr_o[...] = bk.astype(BF16)
    bv = grp(_C_BV, _C_IQ)[:, :DSA_DIM]
    bvf_o[...] = bv
    bvb_o[...] = bv.astype(BF16)
    iq_o[...] = grp(_C_IQ, _C_IK).astype(BF16)
    ik = _rms_groups(grp(_C_IK, _C_IW), a32x8_ref[...], gik_ref[...])
    ikf_o[...] = ik[:, :IDX_DIM]
    ikr_o[...] = ik.astype(BF16)
    iw_o[...] = (grp(_C_IW, _C_QA) * (1.0 / 16.0))[:, :8]

    cq = _rms_rows(grp(_C_QA, _C_KV), gqa_ref[...]).astype(BF16)
    cosq, sinq = cosq_ref[...], sinq_ref[...]
    for h in range(MLA_HEADS):
        sl = slice(h * LANE, (h + 1) * LANE)
        qh = _rms_groups(_dot(cq, wqb_ref[:, sl]), aqc_ref[...], gqc_ref[:, sl])
        qc_o[:, sl] = (qh * cosq + _dot(qh.astype(BF16), rq_ref[...]) * sinq).astype(BF16)
    ckv_o[...] = _rms_rows(grp(_C_KV, _C_PE), gkv_ref[...])
    kp = _rms_groups(grp(_C_PE, _C_END), ape_ref[...], gkr_ref[...])
    kp = kp * cosk_ref[...] + _dot_hilo(kp, rk_ref[...]) * sink_ref[...]
    kpe_o[...] = kp[:, :MLA_ROPE]


def _proj_call(x2d, consts, tabs, tm, n_tab_blocks):
    n = x2d.shape[0]
    row = lambda w: pl.BlockSpec((tm, w), lambda i: (i, 0))
    tab = pl.BlockSpec((tm, LANE), lambda i: (i % n_tab_blocks, 0))
    in_specs = [row(x2d.shape[1])] + [_const_spec(c.shape) for c in consts] + [tab] * 4
    widths = [(FOX_W, BF16), (FOX_W, F32), (FOX_W, BF16), (FOX_W, F32), (FOX_W, BF16), (8, F32),
              (DSA_W, BF16), (DSA_DIM, F32), (DSA_W, BF16), (DSA_DIM, F32), (DSA_DIM, BF16),
              (IDX_W, BF16), (IDX_DIM, F32), (IDX_W, BF16), (8, F32),
              (MLA_HEADS * LANE, BF16), (MLA_KV_LORA, F32), (MLA_ROPE, F32)]
    return pl.pallas_call(
        _proj_kernel,
        grid=(n // tm,),
        in_specs=in_specs,
        out_specs=[row(w) for w, _ in widths],
        out_shape=[jax.ShapeDtypeStruct((n, w), d) for w, d in widths],
        compiler_params=_params(("parallel",)),
        name="proj",
    )(x2d, *consts, *tabs)


def _cumsum_kernel(x_ref, tri_ref, o_ref):
    nblk = x_ref.shape[2] // LANE

    def body(j, carry):
        s = pl.multiple_of(j * LANE, LANE)
        x = x_ref[0, :, pl.ds(s, LANE)]
        x1 = x.astype(BF16)
        r1 = x - x1.astype(F32)
        x2 = r1.astype(BF16)
        x3 = (r1 - x2.astype(F32)).astype(BF16)
        tri = tri_ref[...]
        out = _dot(x1, tri) + _dot(x2, tri) + _dot(x3, tri) + carry
        c = out * LOG2E
        c1 = c.astype(BF16).astype(F32)
        c2 = (c - c1).astype(BF16).astype(F32)
        c3 = (c - c1 - c2).astype(BF16).astype(F32)
        o_ref[0, 0:8, pl.ds(s, LANE)] = c1
        o_ref[0, 8:16, pl.ds(s, LANE)] = c2
        o_ref[0, 16:24, pl.ds(s, LANE)] = c3
        return out[:, LANE - 1:LANE]

    lax.fori_loop(0, nblk, body, jnp.zeros((8, 1), F32))


def _cumsum_call(lf_t):
    b, _, s = lf_t.shape
    tri = jnp.asarray(np.triu(np.ones((LANE, LANE), np.float32)), BF16)
    return pl.pallas_call(
        _cumsum_kernel,
        grid=(b,),
        in_specs=[pl.BlockSpec((1, 8, s), lambda i: (i, 0, 0)), _const_spec((LANE, LANE))],
        out_specs=pl.BlockSpec((1, 24, s), lambda i: (i, 0, 0)),
        out_shape=jax.ShapeDtypeStruct((b, 24, s), F32),
        compiler_params=_params(("parallel",)),
        name="cumsum",
    )(lf_t, tri)


def _softmax_steps(states, logits, vts):
    mid = []
    for (m, acc), s in zip(states, logits):
        m_new = jnp.maximum(m, jnp.max(s, axis=0, keepdims=True))
        mid.append((m_new, jnp.exp2(m - m_new) * acc, jnp.exp2(s - m_new).astype(BF16)))
    return tuple((m_new, acc + _dot(vt, p)) for (m_new, acc, p), vt in zip(mid, vts))


def _init_state(n_heads, tq):
    return tuple((jnp.full((1, tq), M_INIT, F32), jnp.zeros((V_ROWS, tq), F32)) for _ in range(n_heads))


def _finish(state, o_ref):
    out_t = jnp.concatenate([acc[:64] / acc[64:65] for _, acc in state], axis=0)
    o_ref[0] = out_t.T.astype(o_ref.dtype)


def _with_ones_row(v, heads):
    b, s, _ = v.shape
    vt = jnp.swapaxes(v.reshape(b, s, heads, 64), 1, 3)
    vt = jnp.swapaxes(vt, 1, 2)
    extra = jnp.zeros((b, heads, V_ROWS - 64, s), v.dtype).at[:, :, 0].set(1)
    return jnp.concatenate([vt, extra], axis=2).reshape(b, heads * V_ROWS, s)


def _sweep(n_small, step, carry):
    ratio = KB_BIG // KB_SMALL
    n_big = n_small // ratio

    def big(j, c):
        return step(pl.multiple_of(j * KB_BIG, KB_BIG), KB_BIG, c)

    def small(j, c):
        return step(pl.multiple_of(j * KB_SMALL, KB_SMALL), KB_SMALL, c)

    carry = lax.fori_loop(0, n_big, big, carry)
    return lax.fori_loop(n_big * ratio, n_small, small, carry)


def _fox_kernel(q_ref, k_ref, vt_ref, o_ref, *, tq, past):
    q0 = past + pl.program_id(1) * tq
    q_heads = [q_ref[0, :, h * LANE:(h + 1) * LANE] for h in range(FOX_HEADS)]

    def step(ks, width, state, masked=False):
        if masked:
            ok = (ks + lax.broadcasted_iota(jnp.int32, (width, tq), 0)
                  <= q0 + lax.broadcasted_iota(jnp.int32, (width, tq), 1))
        logits = []
        for h in range(FOX_HEADS):
            s = _dot_nt(k_ref[0, pl.ds(ks, width), h * LANE:(h + 1) * LANE], q_heads[h])
            logits.append(jnp.where(ok, s, MASKED) if masked else s)
        vts = [vt_ref[0, h * V_ROWS:(h + 1) * V_ROWS, pl.ds(ks, width)] for h in range(FOX_HEADS)]
        return _softmax_steps(state, logits, vts)

    n_free = q0 // KB_SMALL
    state = _sweep(n_free, step, _init_state(FOX_HEADS, tq))
    state = step(pl.multiple_of(n_free * KB_SMALL, KB_SMALL), KB_SMALL, state, masked=True)
    _finish(state, o_ref)


def _fox_call(q, k, vt, tq, past):
    b, t, _ = q.shape
    s = k.shape[1]
    return pl.pallas_call(
        functools.partial(_fox_kernel, tq=tq, past=past),
        grid=(b, t // tq),
        in_specs=[
            pl.BlockSpec((1, tq, FOX_HEADS * LANE), lambda i, j: (i, j, 0)),
            pl.BlockSpec((1, s, FOX_HEADS * LANE), lambda i, j: (i, 0, 0)),
            pl.BlockSpec((1, FOX_HEADS * V_ROWS, s), lambda i, j: (i, 0, 0)),
        ],
        out_specs=pl.BlockSpec((1, tq, FOX_W), lambda i, j: (i, j, 0)),
        out_shape=jax.ShapeDtypeStruct((b, t, FOX_W), BF16),
        compiler_params=_params(("parallel", "parallel")),
        name="fox",
    )(q, k, vt)


def _fox_operand(x, limbs, sign):
    b, n, _ = x.shape
    limbs = limbs * jnp.asarray(sign, BF16)
    ones = jnp.ones_like(limbs)
    extra = [limbs, ones] if sign > 0 else [ones, limbs]
    zeros = jnp.zeros((b, n, FOX_HEADS, LANE - FOX_DIM - 6), BF16)
    return jnp.concatenate([x.reshape(b, n, FOX_HEADS, FOX_DIM)] + extra + [zeros],
                           axis=-1).reshape(b, n, FOX_HEADS * LANE)


def _mla_kernel(q_ref, k_ref, vt_ref, o_ref, *, tq, past, s_valid):
    q0 = past + pl.program_id(1) * tq
    q_heads = [q_ref[0, :, h * LANE:(h + 1) * LANE] for h in range(MLA_HEADS)]

    def step(ks, width, state, masked=False):
        if masked:
            kpos = ks + lax.broadcasted_iota(jnp.int32, (width, tq), 0)
            qchunk = (q0 + lax.broadcasted_iota(jnp.int32, (width, tq), 1)) // CHUNK
            ok = (kpos // CHUNK <= qchunk) & (kpos < s_valid)
        logits = []
        for h in range(MLA_HEADS):
            s = _dot_nt(k_ref[0, pl.ds(ks, width), h * LANE:(h + 1) * LANE], q_heads[h])
            logits.append(jnp.where(ok, s, MASKED) if masked else s)
        vts = [vt_ref[0, h * V_ROWS:(h + 1) * V_ROWS, pl.ds(ks, width)] for h in range(MLA_HEADS)]
        return _softmax_steps(state, logits, vts)

    n_free = q0 // KB_SMALL
    state = _sweep(n_free, step, _init_state(MLA_HEADS, tq))
    state = step(pl.multiple_of(n_free * KB_SMALL, KB_SMALL), KB_SMALL, state, masked=True)
    _finish(state, o_ref)


def _mla_call(q, k, vt, tq, past, s_valid):
    b, t, _ = q.shape
    s = k.shape[1]
    return pl.pallas_call(
        functools.partial(_mla_kernel, tq=tq, past=past, s_valid=s_valid),
        grid=(b, t // tq),
        in_specs=[
            pl.BlockSpec((1, tq, MLA_HEADS * LANE), lambda i, j: (i, j, 0)),
            pl.BlockSpec((1, s, MLA_HEADS * LANE), lambda i, j: (i, 0, 0)),
            pl.BlockSpec((1, MLA_HEADS * V_ROWS, s), lambda i, j: (i, 0, 0)),
        ],
        out_specs=pl.BlockSpec((1, tq, MLA_W), lambda i, j: (i, j, 0)),
        out_shape=jax.ShapeDtypeStruct((b, t, MLA_W), BF16),
        compiler_params=_params(("parallel", "parallel")),
        name="mla",
    )(q, k, vt)


def _kvup_kernel(ckv_ref, kpe_ref, wk_ref, wv_ref, gkn_ref, akn_ref, place_ref, k_o, v_o):
    cb = ckv_ref[...].astype(BF16)
    v_o[...] = _dot(cb, wv_ref[...]).astype(BF16)
    pe = _dot(kpe_ref[...], place_ref[...])
    for h in range(MLA_HEADS):
        sl = slice(h * LANE, (h + 1) * LANE)
        kn = _rms_groups(_dot(cb, wk_ref[:, sl]), akn_ref[...], gkn_ref[...])
        k_o[:, sl] = (kn + pe).astype(BF16)


def _kvup_call(ckv2d, kpe2d, consts, tm):
    n = ckv2d.shape[0]
    row = lambda w: pl.BlockSpec((tm, w), lambda i: (i, 0))
    return pl.pallas_call(
        _kvup_kernel,
        grid=(n // tm,),
        in_specs=[row(MLA_KV_LORA), row(LANE)] + [_const_spec(c.shape) for c in consts],
        out_specs=[row(MLA_HEADS * LANE), row(MLA_W)],
        out_shape=[jax.ShapeDtypeStruct((n, MLA_HEADS * LANE), BF16),
                   jax.ShapeDtypeStruct((n, MLA_W), BF16)],
        compiler_params=_params(("parallel",)),
        name="kvup",
    )(ckv2d, kpe2d, *consts)


def _dsa_kernel(far_ref, iq_ref, iw_ref, ik_ref, bq_ref, bk_ref, bvt_ref, bias_ref, low_ref,
                o_ref, keys_ref, half_ref, *, tq, past, s_valid, k_sel):
    kb = KB_BIG
    q0 = past + pl.program_id(1) * tq
    nkb = (jnp.minimum(q0 + tq, s_valid) + kb - 1) // kb
    n_far = jnp.maximum(nkb - 2, 0)
    qpos = q0 + lax.broadcasted_iota(jnp.int32, (kb, tq), 1)
    koff = lax.broadcasted_iota(jnp.int32, (kb, tq), 0)
    qlane = lax.broadcasted_iota(jnp.int32, (tq, IDX_W), 1)

    iq = iq_ref[0]
    iq_heads = [jnp.where(qlane // IDX_DIM == h, iq, jnp.zeros_like(iq)) for h in range(IDX_HEADS)]

    def score_body(j, _, masked):
        ks = pl.multiple_of(j * kb, kb)
        ikb = ik_ref[0, pl.ds(ks, kb), :]
        acc = jnp.zeros((kb, tq), F32)
        for h in range(IDX_HEADS):
            acc = acc + jnp.maximum(_dot_nt(ikb, iq_heads[h]), 0.0) * iw_ref[0, h:h + 1, :]
        bits = pltpu.bitcast(acc, jnp.int32)
        key = jnp.where(bits < 0, bits ^ 0x7FFFFFFF, bits)
        if masked:
            kpos = ks + koff
            key = jnp.where(kpos // CHUNK <= qpos // CHUNK, jnp.where(kpos < s_valid, key, KEY_NEG_INF),
                            KEY_NEG_INF)
        keys_ref[pl.ds(ks, kb), :] = key
        half_ref[pl.ds(ks, kb), :] = (key >> 16).astype(jnp.int16)
        return 0

    lax.fori_loop(0, nkb - 1, functools.partial(score_body, masked=False), 0)
    score_body(nkb - 1, 0, masked=True)

    def count_ge16(cand):
        cand = cand.astype(jnp.int16)

        def body(j, c16):
            ks = pl.multiple_of(j * kb, kb)
            hit = jnp.where(half_ref[pl.ds(ks, kb), :] >= cand, jnp.int16(1), jnp.int16(0))
            parts = [hit[i * 16:(i + 1) * 16] for i in range(kb // 16)]
            while len(parts) > 1:
                parts = [a + b for a, b in zip(parts[::2], parts[1::2])]
            return c16 + parts[0]
        c16 = lax.fori_loop(0, nkb, body, jnp.zeros((16, tq), jnp.int16))
        return jnp.sum(c16.astype(jnp.int32), axis=0, keepdims=True)

    def count_gt16(t):
        return jnp.where(t < I16_MAX, count_ge16(jnp.minimum(t + 1, I16_MAX)), 0)

    def kth_largest16(rank):
        def body(i, ub):
            c = ub | jnp.left_shift(jnp.int32(1), 15 - i)
            return jnp.where(count_ge16(c + I16_MIN) >= rank, c, ub)
        return lax.fori_loop(0, 16, body, jnp.zeros((1, tq), jnp.int32)) + I16_MIN

    t_hi = kth_largest16(k_sel)
    above = count_gt16(t_hi)

    def low_half_body(j, _):
        ks = pl.multiple_of(j * kb, kb)
        key = keys_ref[pl.ds(ks, kb), :]
        low = jnp.where((key >> 16) == t_hi, (key & 0xFFFF) + I16_MIN, I16_MIN)
        half_ref[pl.ds(ks, kb), :] = low.astype(jnp.int16)
        return 0

    lax.fori_loop(0, nkb, low_half_body, 0)
    t_lo = kth_largest16(k_sel - above)
    thr = t_hi * 65536 + (t_lo - I16_MIN)
    need = (k_sel - above - count_gt16(t_lo)).astype(F32)
    thr_next = thr + 1

    bq = bq_ref[0]
    blane = lax.broadcasted_iota(jnp.int32, (tq, DSA_W), 1)
    bq_heads = [jnp.where(blane // DSA_DIM == h, bq, jnp.zeros_like(bq)) for h in range(DSA_HEADS)]

    def att_body(j, carry, near):
        seen, state = carry
        ks = pl.multiple_of(j * kb, kb)
        key = keys_ref[pl.ds(ks, kb), :]
        tie = jnp.where(key == thr, 1.0, 0.0)
        rank = seen + _dot(low_ref[...], tie.astype(BF16))
        sel = key >= jnp.maximum(jnp.where(rank < need, thr, thr_next), KEY_NEG_INF + 1)
        bkb = bk_ref[0, pl.ds(ks, kb), :]
        vt = bvt_ref[0, :, pl.ds(ks, kb)]
        if near:
            bs = pl.multiple_of(ks - (q0 - BIAS_BACK), KB_SMALL)
        logits = []
        for h in range(DSA_HEADS):
            s = _dot_nt(bkb, bq_heads[h])
            s = s + (bias_ref[h, pl.ds(bs, kb), :] if near else far_ref[h])
            logits.append(jnp.where(sel, s, MASKED))
        return (seen + jnp.sum(tie, axis=0, keepdims=True),
                _softmax_steps(state, logits, [vt] * DSA_HEADS))

    carry = (jnp.zeros((1, tq), F32), _init_state(DSA_HEADS, tq))
    carry = lax.fori_loop(0, n_far, functools.partial(att_body, near=False), carry)
    carry = lax.fori_loop(n_far, nkb, functools.partial(att_body, near=True), carry)
    _finish(carry[1], o_ref)


def _dsa_call(far, iq, iw_t, ik, bq, bk, bv_t, bias_t, tq, past, s_valid, k_sel):
    b, t, _ = iq.shape
    s = ik.shape[1]
    low = jnp.asarray(np.tril(np.ones((KB_BIG, KB_BIG), np.float32), -1), BF16)
    return pl.pallas_call(
        functools.partial(_dsa_kernel, tq=tq, past=past, s_valid=s_valid, k_sel=k_sel),
        grid=(b, t // tq),
        in_specs=[
            pl.BlockSpec(memory_space=pltpu.SMEM),
            pl.BlockSpec((1, tq, IDX_W), lambda i, j: (i, j, 0)),
            pl.BlockSpec((1, 8, tq), lambda i, j: (i, 0, j)),
            pl.BlockSpec((1, s, IDX_W), lambda i, j: (i, 0, 0)),
            pl.BlockSpec((1, tq, DSA_W), lambda i, j: (i, j, 0)),
            pl.BlockSpec((1, s, DSA_W), lambda i, j: (i, 0, 0)),
            pl.BlockSpec((1, V_ROWS, s), lambda i, j: (i, 0, 0)),
            _const_spec(bias_t.shape),
            _const_spec(low.shape),
        ],
        out_specs=pl.BlockSpec((1, tq, DSA_W), lambda i, j: (i, j, 0)),
        out_shape=jax.ShapeDtypeStruct((b, t, DSA_W), BF16),
        scratch_shapes=[pltpu.VMEM((s, tq), jnp.int32), pltpu.VMEM((s, tq), jnp.int16)],
        compiler_params=_params(("parallel", "parallel")),
        name="dsa",
    )(far, iq, iw_t, ik, bq, bk, bv_t, bias_t, low)


def _merge_kernel(x_ref, ya_ref, yb_ref, yc_ref, g_ref, wg_ref, wa_ref, wb_ref, wc_ref, wo_ref, o_ref):
    x = x_ref[...]
    d = x.shape[1]
    hb = _rms_rows(x, g_ref[...]).astype(BF16)
    mix = jnp.zeros_like(x)
    for i, (y_ref, w_ref) in enumerate(((ya_ref, wa_ref), (yb_ref, wb_ref), (yc_ref, wc_ref))):
        gate = jax.nn.sigmoid(_dot(hb, wg_ref[:, i * d:(i + 1) * d]))
        mix = mix + gate * _dot(y_ref[...], w_ref[...])
    o_ref[...] = x + _dot(mix.astype(BF16), wo_ref[...])


def _merge_call(x2d, ya, yb, yc, consts, tm):
    n, d = x2d.shape
    row = lambda w: pl.BlockSpec((tm, w), lambda i: (i, 0))
    return pl.pallas_call(
        _merge_kernel,
        grid=(n // tm,),
        in_specs=[row(d), row(FOX_W), row(DSA_W), row(MLA_W)] + [_const_spec(c.shape) for c in consts],
        out_specs=row(d),
        out_shape=jax.ShapeDtypeStruct((n, d), F32),
        compiler_params=_params(("parallel",)),
        name="merge",
    )(x2d, ya, yb, yc, *consts)


def _ffn_kernel(x_ref, g_ref, wi_ref, wo_ref, o_ref, *, n_chunks):
    x = x_ref[...]
    hb = _rms_rows(x, g_ref[...]).astype(BF16)
    out = x
    for c in range(n_chunks):
        gu = _dot(hb, wi_ref[c])
        half = gu.shape[1] // 2
        gt, up = gu[:, :half], gu[:, half:]
        out = out + _dot((gt * jax.nn.sigmoid(gt) * up).astype(BF16), wo_ref[c])
    o_ref[...] = out


def _ffn_call(x2d, g, wi, wo, tm):
    n, d = x2d.shape
    row = pl.BlockSpec((tm, d), lambda i: (i, 0))
    return pl.pallas_call(
        functools.partial(_ffn_kernel, n_chunks=wi.shape[0]),
        grid=(n // tm,),
        in_specs=[row, _const_spec(g.shape), _const_spec(wi.shape), _const_spec(wo.shape)],
        out_specs=row,
        out_shape=jax.ShapeDtypeStruct((n, d), F32),
        compiler_params=_params(("parallel",)),
        name="ffn",
    )(x2d, g, wi, wo)


def _block_avg(blocks, width):
    m = np.zeros((width, width), np.float32)
    for lo, hi in blocks:
        m[lo:hi, lo:hi] = 1.0 / (hi - lo)
    return jnp.asarray(m, BF16)


def _rot_matrix(base):
    r = np.zeros((LANE, LANE), np.float32)
    half = MLA_ROPE // 2
    for i in range(half):
        r[base + half + i, base + i] = -1.0
        r[base + i, base + half + i] = 1.0
    return jnp.asarray(r, BF16)


def _rope_tables(pos, base):
    half = MLA_ROPE // 2
    freq = ROPE_THETA ** (-jnp.arange(half, dtype=F32) / half)
    ang = pos.astype(F32)[:, None] * freq[None, :]
    cos, sin = jnp.cos(ang), jnp.sin(ang)
    n = pos.shape[0]
    ct = jnp.ones((n, LANE), F32).at[:, base:base + MLA_ROPE].set(jnp.concatenate([cos, cos], 1))
    st = jnp.zeros((n, LANE), F32).at[:, base:base + MLA_ROPE].set(jnp.concatenate([sin, sin], 1))
    return ct, st


def _t5_bucket(rel):
    nb = REL_BUCKETS // 2
    max_exact = nb // 2
    side = jnp.where(rel > 0, nb, 0)
    n = jnp.abs(rel)
    large = max_exact + (jnp.log(jnp.maximum(n, 1).astype(F32) / max_exact)
                         / math.log(REL_MAX_DIST / max_exact) * (nb - max_exact)).astype(jnp.int32)
    large = jnp.minimum(large, nb - 1)
    return side + jnp.where(n < max_exact, n, large)


def _bias_tables(rel_bias, tq):
    c = jnp.arange(BIAS_ROWS, dtype=jnp.int32)[:, None]
    r = jnp.arange(tq, dtype=jnp.int32)[None, :]
    onehot = (_t5_bucket(c - BIAS_BACK - r)[None] == jnp.arange(REL_BUCKETS)[:, None, None]).astype(F32)
    near = jnp.einsum('bcr,bh->hcr', onehot, rel_bias.astype(F32), precision=lax.Precision.HIGHEST) * LOG2E
    far = rel_bias[_t5_bucket(jnp.int32(-REL_MAX_DIST))].astype(F32) * LOG2E
    return near, far


def _pad_cols(w, width):
    return jnp.pad(w, ((0, 0), (0, width - w.shape[1])))


def _tile_vec(g, reps, scale=1.0):
    return (jnp.tile(g.astype(F32), reps) * scale)[None, :]


def _layer_consts(p):
    splits = np.cumsum([FOX_W, FOX_W, FOX_W, FOX_HEADS, DSA_W, DSA_DIM, DSA_DIM, IDX_W, IDX_DIM, IDX_HEADS,
                        MLA_Q_LORA, MLA_KV_LORA, MLA_ROPE])[:-1]
    fq, fk, fv, fg, bq, bk, bv, iq, ik, iw, cqa, ckva, ckpe = jnp.split(p['w_in'], [int(v) for v in splits], axis=1)
    w_all = jnp.concatenate([
        fq, fk, fv, _pad_cols(fg, LANE), bq, jnp.tile(bk, (1, DSA_HEADS)), _pad_cols(bv, LANE),
        iq, jnp.tile(ik, (1, IDX_HEADS)), _pad_cols(iw, LANE), cqa, ckva, _pad_cols(ckpe, LANE)],
        axis=1).astype(BF16)
    assert w_all.shape[1] == _C_END
    d_qk = MLA_NOPE + MLA_ROPE
    wqb = p['mla_wqb'].reshape(MLA_Q_LORA, MLA_HEADS, d_qk)
    wqb = jnp.pad(wqb, ((0, 0), (0, 0), (0, LANE - d_qk))).reshape(MLA_Q_LORA, MLA_HEADS * LANE).astype(BF16)
    gqc = jnp.concatenate([p['mla_gqn'], p['mla_gqr'], jnp.zeros((LANE - d_qk,), F32)]).astype(F32)
    proj = [
        p['norm_mix'].astype(F32)[None, :], w_all,
        _pad_cols(p['fox_bf'].astype(F32)[None, :], LANE),
        _tile_vec(p['fox_gq'], FOX_HEADS, FOX_DIM ** -0.5 * LOG2E), _tile_vec(p['fox_gk'], FOX_HEADS),
        _tile_vec(p['dsa_gq'], DSA_HEADS, DSA_DIM ** -0.5 * LOG2E), _tile_vec(p['dsa_gk'], DSA_HEADS),
        _tile_vec(p['idx_gk'], IDX_HEADS),
        p['mla_gqa'].astype(F32)[None, :], p['mla_gkv'].astype(F32)[None, :],
        _pad_cols(p['mla_gkr'].astype(F32)[None, :], LANE),
        wqb, _tile_vec(gqc, MLA_HEADS, d_qk ** -0.5 * LOG2E),
        _block_avg([(i * 64, i * 64 + 64) for i in range(6)], FOX_W),
        _block_avg([(i * 64, i * 64 + 64) for i in range(4)], DSA_W),
        _block_avg([(i * 32, i * 32 + 32) for i in range(8)], IDX_W),
        _block_avg([(0, MLA_NOPE), (MLA_NOPE, d_qk)], LANE),
        _block_avg([(0, MLA_ROPE)], LANE),
        _rot_matrix(MLA_NOPE), _rot_matrix(0),
    ]
    wkvb = p['mla_wkvb'].reshape(MLA_KV_LORA, MLA_HEADS, MLA_NOPE + MLA_V)
    wk = jnp.pad(wkvb[:, :, :MLA_NOPE], ((0, 0), (0, 0), (0, LANE - MLA_NOPE)))
    wk = wk.reshape(MLA_KV_LORA, MLA_HEADS * LANE).astype(BF16)
    wv = wkvb[:, :, MLA_NOPE:].reshape(MLA_KV_LORA, MLA_W).astype(BF16)
    place = np.zeros((LANE, LANE), np.float32)
    place[np.arange(MLA_ROPE), MLA_NOPE + np.arange(MLA_ROPE)] = 1.0
    kvup = [wk, wv, _pad_cols(p['mla_gkn'].astype(F32)[None, :], LANE),
            _block_avg([(0, MLA_NOPE)], LANE), jnp.asarray(place, BF16)]
    merge = [p['norm_mix'].astype(F32)[None, :], p['w_gate'].astype(BF16), p['w_fox_out'].astype(BF16),
             p['w_dsa_out'].astype(BF16), p['w_mla_out'].astype(BF16), p['w_o'].astype(BF16)]
    d_ff = p['w_ffn_out'].shape[0]
    n_chunks = 2
    ck = d_ff // n_chunks
    wi = p['w_ffn_in']
    wi = jnp.stack([jnp.concatenate([wi[:, c * ck:(c + 1) * ck], wi[:, d_ff + c * ck:d_ff + (c + 1) * ck]], 1)
                    for c in range(n_chunks)]).astype(BF16)
    wo = p['w_ffn_out'].reshape(n_chunks, ck, -1).astype(BF16)
    ffn = [p['norm_ffn'].astype(F32)[None, :], wi, wo]
    return proj, kvup, merge, ffn


def _round_up(v, m):
    return -(-v // m) * m


def _layer(x, past, p, rel_bias, bias_cache):
    b, t, d = x.shape
    pl_len = 0 if past is None else past[0].shape[1]
    s_valid = pl_len + t
    n = b * t
    tm = min(512, n)
    t_att = _round_up(t, LANE)
    tq = min(256, t_att)
    assert pl_len % KB_SMALL == 0 and t_att % tq == 0
    s_pad = _round_up(pl_len + t_att - tq + max(tq, KB_SMALL), KB_BIG)
    k_sel = min(TOPK_MAX, s_valid // 4)
    proj_c, kvup_c, merge_c, ffn_c = _layer_consts(p)

    pos = pl_len + jnp.arange(t, dtype=jnp.int32)
    tabs = _rope_tables(pos, MLA_NOPE) + _rope_tables(pos, 0)
    if t < tm:
        tabs = tuple(jnp.tile(a, (tm // t, 1)) for a in tabs)
    n_tab_blocks = max(t // tm, 1)

    x2d = x.reshape(n, d)
    (fq, fk_f, fk_b, fv_f, fv_b, lf8, bq, bk_f, bk_r, bv_f, bv_b, iq, ik_f, ik_r, iw8, qc, ckv_f, kpe_f
     ) = _proj_call(x2d, proj_c, tabs, tm, n_tab_blocks)

    def seq(a):
        return a.reshape(b, t, a.shape[-1])

    def queries(a):
        return jnp.pad(seq(a), ((0, 0), (0, t_att - t), (0, 0)))

    def with_past(cached, new, dtype):
        new = seq(new).astype(dtype)
        if past is not None:
            new = jnp.concatenate([cached.reshape(b, pl_len, -1).astype(dtype), new], axis=1)
        return jnp.pad(new, ((0, 0), (0, s_pad - s_valid), (0, 0)))

    rows = (seq(fk_f).reshape(b, t, FOX_HEADS, FOX_DIM), seq(fv_f).reshape(b, t, FOX_HEADS, FOX_DIM),
            seq(lf8)[:, :, :FOX_HEADS], seq(bk_f), seq(bv_f), seq(ik_f), seq(ckv_f), seq(kpe_f))
    cache = (None,) * 8 if past is None else past

    lf_all = with_past(None if past is None else jnp.pad(cache[2], ((0, 0), (0, 0), (0, 2))), lf8, F32)
    cum = _cumsum_call(jnp.swapaxes(lf_all, 1, 2)).reshape(b, 3, 8, s_pad)[:, :, :FOX_HEADS]
    cum = jnp.transpose(cum, (0, 3, 2, 1)).astype(BF16)
    ya = _fox_call(_fox_operand(queries(fq), cum[:, pl_len:pl_len + t_att], 1),
                   _fox_operand(with_past(cache[0], fk_b, BF16), cum, -1),
                   _with_ones_row(with_past(cache[1], fv_b, BF16), FOX_HEADS), tq, pl_len)[:, :t]

    def rep(a, k):
        return jnp.tile(a.reshape(b, pl_len, -1), (1, 1, k))

    if tq not in bias_cache:
        bias_cache[tq] = _bias_tables(rel_bias, tq)
    bias_t, far = bias_cache[tq]
    ik_all = with_past(None if past is None else rep(cache[5], IDX_HEADS), ik_r, BF16)
    bk_all = with_past(None if past is None else rep(cache[3], DSA_HEADS), bk_r, BF16)
    bv_t = _with_ones_row(with_past(cache[4], bv_b, BF16), 1)
    iw_t = jnp.swapaxes(queries(iw8), 1, 2)
    yb = _dsa_call(far, queries(iq), iw_t, ik_all, queries(bq), bk_all, bv_t, bias_t,
                   tq, pl_len, s_valid, k_sel)[:, :t]

    ckv_all = with_past(cache[6], ckv_f, F32).reshape(b * s_pad, MLA_KV_LORA)
    kpe_all = with_past(cache[7], kpe_f, BF16)
    kpe_all = jnp.pad(kpe_all, ((0, 0), (0, 0), (0, LANE - MLA_ROPE))).reshape(b * s_pad, LANE)
    kc, mv = _kvup_call(ckv_all, kpe_all, kvup_c, min(512, b * s_pad))
    yc = _mla_call(queries(qc), kc.reshape(b, s_pad, -1), _with_ones_row(mv.reshape(b, s_pad, -1), MLA_HEADS),
                   tq, pl_len, s_valid)[:, :t]

    x1 = _merge_call(x2d, ya.reshape(n, -1), yb.reshape(n, -1), yc.reshape(n, -1), merge_c, tm)
    x2 = _ffn_call(x1, *ffn_c, tm)
    return x2.reshape(b, t, d), rows


def kernel(x_prompt, x_sample, cache_fox_k, cache_fox_v, cache_fox_logf, cache_dsa_k, cache_dsa_v, cache_idx_k, cache_mla_ckv, cache_mla_kpe, rel_bias, norm_mix, w_in, fox_gq, fox_gk, fox_bf, dsa_gq, dsa_gk, idx_gk, mla_gqa, mla_wqb, mla_gqn, mla_gqr, mla_gkv, mla_gkr, mla_wkvb, mla_gkn, w_fox_out, w_dsa_out, w_mla_out, w_gate, w_o, norm_ffn, w_ffn_in, w_ffn_out):
    caches = (cache_fox_k, cache_fox_v, cache_fox_logf, cache_dsa_k, cache_dsa_v,
              cache_idx_k, cache_mla_ckv, cache_mla_kpe)
    yp, ys = x_prompt, x_sample
    p_rows, s_rows, bias_cache = [], [], {}
    for i in range(norm_mix.shape[0]):
        p = dict(norm_mix=norm_mix[i], w_in=w_in[i], fox_gq=fox_gq[i], fox_gk=fox_gk[i],
                 fox_bf=fox_bf[i], dsa_gq=dsa_gq[i], dsa_gk=dsa_gk[i], idx_gk=idx_gk[i],
                 mla_gqa=mla_gqa[i], mla_wqb=mla_wqb[i], mla_gqn=mla_gqn[i], mla_gqr=mla_gqr[i],
                 mla_gkv=mla_gkv[i], mla_gkr=mla_gkr[i], mla_wkvb=mla_wkvb[i], mla_gkn=mla_gkn[i],
                 w_fox_out=w_fox_out[i], w_dsa_out=w_dsa_out[i], w_mla_out=w_mla_out[i],
                 w_gate=w_gate[i], w_o=w_o[i], norm_ffn=norm_ffn[i], w_ffn_in=w_ffn_in[i],
                 w_ffn_out=w_ffn_out[i])
        yp, rows_p = _layer(yp, None, p, rel_bias, bias_cache)
        ys, rows_s = _layer(ys, tuple(c[i] for c in caches), p, rel_bias, bias_cache)
        p_rows.append(rows_p)
        s_rows.append(rows_s)

    def st(rows, j):
        return jnp.stack([r[j] for r in rows], axis=0)

    return ((yp, ys) + tuple(st(p_rows, j) for j in range(8)) + tuple(st(s_rows, j) for j in range(8)))
```

```python
import functools
import math

import numpy as np
import jax
import jax.numpy as jnp
from jax import lax
from jax.experimental import pallas as pl
from jax.experimental.pallas import tpu as pltpu

F32 = jnp.float32
BF16 = jnp.bfloat16

CHUNK = 64
EPS = 1e-6
FOX_HEADS, FOX_DIM = 6, 64
FOX_W = FOX_HEADS * FOX_DIM
DSA_HEADS, DSA_DIM = 4, 64
DSA_W = DSA_HEADS * DSA_DIM
IDX_HEADS, IDX_DIM = 8, 32
IDX_W = IDX_HEADS * IDX_DIM
TOPK_MAX = 256
MLA_HEADS = 6
MLA_Q_LORA, MLA_KV_LORA = 256, 128
MLA_NOPE, MLA_ROPE, MLA_V = 64, 32, 64
MLA_W = MLA_HEADS * MLA_V
ROPE_THETA = 10000.0
REL_BUCKETS, REL_MAX_DIST = 32, 128

LANE = 128
KB_SMALL, KB_BIG = 256, 512
BIAS_BACK = 768
BIAS_ROWS = BIAS_BACK + KB_BIG
LOG2E = math.log2(math.e)
M_INIT = -1e30
MASKED = -3e30
INT_MIN = -2 ** 31
KEY_NEG_INF = -2139095041
I16_MIN, I16_MAX = -2 ** 15, 2 ** 15 - 1
VMEM_LIMIT = 56 * 1024 * 1024

_C_FQ, _C_FK, _C_FV, _C_FG = 0, 384, 768, 1152
_C_BQ, _C_BK, _C_BV = 1280, 1536, 1792
_C_IQ, _C_IK, _C_IW = 1920, 2176, 2432
_C_QA, _C_KV, _C_PE = 2560, 2816, 2944
_C_END = 3072


def _dot(a, b):
    return jnp.dot(a, b, preferred_element_type=F32)


def _dot_nt(a, b):
    return lax.dot_general(a, b, (((1,), (1,)), ((), ())), preferred_element_type=F32)


def _dot_hilo(x, m):
    hi = x.astype(BF16)
    lo = (x - hi.astype(F32)).astype(BF16)
    return _dot(hi, m) + _dot(lo, m)


def _rms_rows(x, g):
    return x * lax.rsqrt(jnp.mean(x * x, axis=-1, keepdims=True) + EPS) * g


def _rms_groups(x, avg, g):
    return x * lax.rsqrt(_dot((x * x).astype(BF16), avg) + EPS) * g


def _params(sem, vmem=VMEM_LIMIT):
    return pltpu.CompilerParams(dimension_semantics=sem, vmem_limit_bytes=vmem)


def _const_spec(shape):
    nd = len(shape)
    return pl.BlockSpec(shape, lambda *_: (0,) * nd)


def _proj_kernel(x_ref, gmix_ref, w_ref, bf_ref, gfq_ref, gfk_ref, gbq_ref, gbk_ref, gik_ref,
                 gqa_ref, gkv_ref, gkr_ref, wqb_ref, gqc_ref,
                 a64x6_ref, a64x4_ref, a32x8_ref, aqc_ref, ape_ref, rq_ref, rk_ref,
                 cosq_ref, sinq_ref, cosk_ref, sink_ref,
                 fq_o, fkf_o, fkb_o, fvf_o, fvb_o, lf_o, bq_o, bkf_o, bkr_o, bvf_o, bvb_o,
                 iq_o, ikf_o, ikr_o, iw_o, qc_o, ckv_o, kpe_o):
    x = x_ref[...]
    hb = _rms_rows(x, gmix_ref[...]).astype(BF16)

    def grp(lo, hi):
        return _dot(hb, w_ref[:, lo:hi])

    fq_o[...] = _rms_groups(grp(_C_FQ, _C_FK), a64x6_ref[...], gfq_ref[...]).astype(BF16)
    fk = _rms_groups(grp(_C_FK, _C_FV), a64x6_ref[...], gfk_ref[...])
    fkf_o[...] = fk
    fkb_o[...] = fk.astype(BF16)
    fv = grp(_C_FV, _C_FG)
    fvf_o[...] = fv
    fvb_o[...] = fv.astype(BF16)
    z = grp(_C_FG, _C_BQ) + bf_ref[...]
    lf = jnp.minimum(z, 0.0) - jnp.log1p(jnp.exp(-jnp.abs(z)))
    lf_o[...] = lf[:, :8]

    bq_o[...] = _rms_groups(grp(_C_BQ, _C_BK), a64x4_ref[...], gbq_ref[...]).astype(BF16)
    bk = _rms_groups(grp(_C_BK, _C_BV), a64x4_ref[...], gbk_ref[...])
    bkf_o[...] = bk[:, :DSA_DIM]
    bkr_o[...] = bk.astype(BF16)
    bv = grp(_C_BV, _C_IQ)[:, :DSA_DIM]
    bvf_o[...] = bv
    bvb_o[...] = bv.astype(BF16)
    iq_o[...] = grp(_C_IQ, _C_IK).astype(BF16)
    ik = _rms_groups(grp(_C_IK, _C_IW), a32x8_ref[...], gik_ref[...])
    ikf_o[...] = ik[:, :IDX_DIM]
    ikr_o[...] = ik.astype(BF16)
    iw_o[...] = (grp(_C_IW, _C_QA) * (1.0 / 16.0))[:, :8]

    cq = _rms_rows(grp(_C_QA, _C_KV), gqa_ref[...]).astype(BF16)
    cosq, sinq = cosq_ref[...], sinq_ref[...]
    for h in range(MLA_HEADS):
        sl = slice(h * LANE, (h + 1) * LANE)
        qh = _rms_groups(_dot(cq, wqb_ref[:, sl]), aqc_ref[...], gqc_ref[:, sl])
        qc_o[:, sl] = (qh * cosq + _dot(qh.astype(BF16), rq_ref[...]) * sinq).astype(BF16)
    ckv_o[...] = _rms_rows(grp(_C_KV, _C_PE), gkv_ref[...])
    kp = _rms_groups(grp(_C_PE, _C_END), ape_ref[...], gkr_ref[...])
    kp = kp * cosk_ref[...] + _dot_hilo(kp, rk_ref[...]) * sink_ref[...]
    kpe_o[...] = kp[:, :MLA_ROPE]


def _proj_call(x2d, consts, tabs, tm, n_tab_blocks):
    n = x2d.shape[0]
    row = lambda w: pl.BlockSpec((tm, w), lambda i: (i, 0))
    tab = pl.BlockSpec((tm, LANE), lambda i: (i % n_tab_blocks, 0))
    in_specs = [row(x2d.shape[1])] + [_const_spec(c.shape) for c in consts] + [tab] * 4
    widths = [(FOX_W, BF16), (FOX_W, F32), (FOX_W, BF16), (FOX_W, F32), (FOX_W, BF16), (8, F32),
              (DSA_W, BF16), (DSA_DIM, F32), (DSA_W, BF16), (DSA_DIM, F32), (DSA_DIM, BF16),
              (IDX_W, BF16), (IDX_DIM, F32), (IDX_W, BF16), (8, F32),
              (MLA_HEADS * LANE, BF16), (MLA_KV_LORA, F32), (MLA_ROPE, F32)]
    return pl.pallas_call(
        _proj_kernel,
        grid=(n // tm,),
        in_specs=in_specs,
        out_specs=[row(w) for w, _ in widths],
        out_shape=[jax.ShapeDtypeStruct((n, w), d) for w, d in widths],
        compiler_params=_params(("parallel",)),
        name="proj",
    )(x2d, *consts, *tabs)


def _cumsum_kernel(x_ref, tri_ref, o_ref):
    nblk = x_ref.shape[2] // LANE

    def body(j, carry):
        s = pl.multiple_of(j * LANE, LANE)
        x = x_ref[0, :, pl.ds(s, LANE)]
        x1 = x.astype(BF16)
        r1 = x - x1.astype(F32)
        x2 = r1.astype(BF16)
        x3 = (r1 - x2.astype(F32)).astype(BF16)
        tri = tri_ref[...]
        out = _dot(x1, tri) + _dot(x2, tri) + _dot(x3, tri) + carry
        o_ref[0, :, pl.ds(s, LANE)] = out * LOG2E
        return out[:, LANE - 1:LANE]

    lax.fori_loop(0, nblk, body, jnp.zeros((8, 1), F32))


def _cumsum_call(lf_t):
    b, _, s = lf_t.shape
    tri = jnp.asarray(np.triu(np.ones((LANE, LANE), np.float32)), BF16)
    return pl.pallas_call(
        _cumsum_kernel,
        grid=(b,),
        in_specs=[pl.BlockSpec((1, 8, s), lambda i: (i, 0, 0)), _const_spec((LANE, LANE))],
        out_specs=pl.BlockSpec((1, 8, s), lambda i: (i, 0, 0)),
        out_shape=jax.ShapeDtypeStruct((b, 8, s), F32),
        compiler_params=_params(("parallel",)),
        name="cumsum",
    )(lf_t, tri)


def _softmax_steps(states, logits, vts):
    mid = []
    for (m, l, acc), s in zip(states, logits):
        m_new = jnp.maximum(m, jnp.max(s, axis=0, keepdims=True))
        alpha = jnp.exp2(m - m_new)
        p = jnp.exp2(s - m_new)
        mid.append((m_new, alpha * l + jnp.sum(p, axis=0, keepdims=True), alpha * acc, p.astype(BF16)))
    return tuple((m_new, l, acc + _dot(vt, p)) for (m_new, l, acc, p), vt in zip(mid, vts))


def _init_state(n_heads, tq):
    return tuple((jnp.full((1, tq), M_INIT, F32), jnp.zeros((1, tq), F32), jnp.zeros((64, tq), F32))
                 for _ in range(n_heads))


def _finish(state, o_ref):
    out_t = jnp.concatenate([acc / l for _, l, acc in state], axis=0)
    o_ref[0] = out_t.T.astype(o_ref.dtype)


def _sweep(n_small, step, carry):
    ratio = KB_BIG // KB_SMALL
    n_big = n_small // ratio

    def big(j, c):
        return step(pl.multiple_of(j * KB_BIG, KB_BIG), KB_BIG, c)

    def small(j, c):
        return step(pl.multiple_of(j * KB_SMALL, KB_SMALL), KB_SMALL, c)

    carry = lax.fori_loop(0, n_big, big, carry)
    return lax.fori_loop(n_big * ratio, n_small, small, carry)


def _fox_kernel(q_ref, k_ref, vt_ref, cq_ref, ck_ref, o_ref, *, tq, past):
    q0 = past + pl.program_id(1) * tq
    lane = lax.broadcasted_iota(jnp.int32, (tq, LANE), 1)
    q_heads, cq = [], []
    for h in range(FOX_HEADS):
        qp = q_ref[0, :, (h // 2) * LANE:(h // 2 + 1) * LANE]
        q_heads.append(jnp.where((lane < 64) == (h % 2 == 0), qp, jnp.zeros_like(qp)))
        cq.append(cq_ref[0, h:h + 1, :])

    def step(ks, width, state, masked=False):
        if masked:
            ok = (ks + lax.broadcasted_iota(jnp.int32, (width, tq), 0)
                  <= q0 + lax.broadcasted_iota(jnp.int32, (width, tq), 1))
        logits = []
        for h in range(FOX_HEADS):
            kb = k_ref[0, pl.ds(ks, width), (h // 2) * LANE:(h // 2 + 1) * LANE]
            s = _dot_nt(kb, q_heads[h]) + cq[h] - ck_ref[0, pl.ds(ks, width), h:h + 1]
            logits.append(jnp.where(ok, s, MASKED) if masked else s)
        vts = [vt_ref[0, h * 64:(h + 1) * 64, pl.ds(ks, width)] for h in range(FOX_HEADS)]
        return _softmax_steps(state, logits, vts)

    n_free = q0 // KB_SMALL
    state = _sweep(n_free, step, _init_state(FOX_HEADS, tq))
    state = step(pl.multiple_of(n_free * KB_SMALL, KB_SMALL), KB_SMALL, state, masked=True)
    _finish(state, o_ref)


def _fox_call(q, k, vt, cq, ck, tq, past):
    b, t, _ = q.shape
    s = k.shape[1]
    return pl.pallas_call(
        functools.partial(_fox_kernel, tq=tq, past=past),
        grid=(b, t // tq),
        in_specs=[
            pl.BlockSpec((1, tq, FOX_W), lambda i, j: (i, j, 0)),
            pl.BlockSpec((1, s, FOX_W), lambda i, j: (i, 0, 0)),
            pl.BlockSpec((1, FOX_W, s), lambda i, j: (i, 0, 0)),
            pl.BlockSpec((1, 8, tq), lambda i, j: (i, 0, j)),
            pl.BlockSpec((1, s, 8), lambda i, j: (i, 0, 0)),
        ],
        out_specs=pl.BlockSpec((1, tq, FOX_W), lambda i, j: (i, j, 0)),
        out_shape=jax.ShapeDtypeStruct((b, t, FOX_W), BF16),
        compiler_params=_params(("parallel", "parallel")),
        name="fox",
    )(q, k, vt, cq, ck)


def _mla_kernel(q_ref, k_ref, vt_ref, o_ref, *, tq, past, s_valid):
    q0 = past + pl.program_id(1) * tq
    q_heads = [q_ref[0, :, h * LANE:(h + 1) * LANE] for h in range(MLA_HEADS)]

    def step(ks, width, state, masked=False):
        if masked:
            kpos = ks + lax.broadcasted_iota(jnp.int32, (width, tq), 0)
            qchunk = (q0 + lax.broadcasted_iota(jnp.int32, (width, tq), 1)) // CHUNK
            ok = (kpos // CHUNK <= qchunk) & (kpos < s_valid)
        logits = []
        for h in range(MLA_HEADS):
            s = _dot_nt(k_ref[0, pl.ds(ks, width), h * LANE:(h + 1) * LANE], q_heads[h])
            logits.append(jnp.where(ok, s, MASKED) if masked else s)
        vts = [vt_ref[0, h * 64:(h + 1) * 64, pl.ds(ks, width)] for h in range(MLA_HEADS)]
        return _softmax_steps(state, logits, vts)

    n_free = q0 // KB_SMALL
    state = _sweep(n_free, step, _init_state(MLA_HEADS, tq))
    state = step(pl.multiple_of(n_free * KB_SMALL, KB_SMALL), KB_SMALL, state, masked=True)
    _finish(state, o_ref)


def _mla_call(q, k, vt, tq, past, s_valid):
    b, t, _ = q.shape
    s = k.shape[1]
    return pl.pallas_call(
        functools.partial(_mla_kernel, tq=tq, past=past, s_valid=s_valid),
        grid=(b, t // tq),
        in_specs=[
            pl.BlockSpec((1, tq, MLA_HEADS * LANE), lambda i, j: (i, j, 0)),
            pl.BlockSpec((1, s, MLA_HEADS * LANE), lambda i, j: (i, 0, 0)),
            pl.BlockSpec((1, MLA_W, s), lambda i, j: (i, 0, 0)),
        ],
        out_specs=pl.BlockSpec((1, tq, MLA_W), lambda i, j: (i, j, 0)),
        out_shape=jax.ShapeDtypeStruct((b, t, MLA_W), BF16),
        compiler_params=_params(("parallel", "parallel")),
        name="mla",
    )(q, k, vt)


def _kvup_kernel(ckv_ref, kpe_ref, wk_ref, wv_ref, gkn_ref, akn_ref, place_ref, k_o, v_o):
    cb = ckv_ref[...].astype(BF16)
    v_o[...] = _dot(cb, wv_ref[...]).astype(BF16)
    pe = _dot(kpe_ref[...], place_ref[...])
    for h in range(MLA_HEADS):
        sl = slice(h * LANE, (h + 1) * LANE)
        kn = _rms_groups(_dot(cb, wk_ref[:, sl]), akn_ref[...], gkn_ref[...])
        k_o[:, sl] = (kn + pe).astype(BF16)


def _kvup_call(ckv2d, kpe2d, consts, tm):
    n = ckv2d.shape[0]
    row = lambda w: pl.BlockSpec((tm, w), lambda i: (i, 0))
    return pl.pallas_call(
        _kvup_kernel,
        grid=(n // tm,),
        in_specs=[row(MLA_KV_LORA), row(LANE)] + [_const_spec(c.shape) for c in consts],
        out_specs=[row(MLA_HEADS * LANE), row(MLA_W)],
        out_shape=[jax.ShapeDtypeStruct((n, MLA_HEADS * LANE), BF16),
                   jax.ShapeDtypeStruct((n, MLA_W), BF16)],
        compiler_params=_params(("parallel",)),
        name="kvup",
    )(ckv2d, kpe2d, *consts)


def _dsa_kernel(far_ref, iq_ref, iw_ref, ik_ref, bq_ref, bk_ref, bvt_ref, bias_ref, low_ref,
                o_ref, keys_ref, half_ref, *, tq, past, s_valid, k_sel):
    kb = KB_BIG
    q0 = past + pl.program_id(1) * tq
    nkb = (jnp.minimum(q0 + tq, s_valid) + kb - 1) // kb
    n_far = jnp.maximum(nkb - 2, 0)
    qpos = q0 + lax.broadcasted_iota(jnp.int32, (kb, tq), 1)
    koff = lax.broadcasted_iota(jnp.int32, (kb, tq), 0)
    qlane = lax.broadcasted_iota(jnp.int32, (tq, IDX_W), 1)

    iq = iq_ref[0]
    iq_heads = [jnp.where(qlane // IDX_DIM == h, iq, jnp.zeros_like(iq)) for h in range(IDX_HEADS)]

    def score_body(j, _, masked):
        ks = pl.multiple_of(j * kb, kb)
        ikb = ik_ref[0, pl.ds(ks, kb), :]
        acc = jnp.zeros((kb, tq), F32)
        for h in range(IDX_HEADS):
            acc = acc + jnp.maximum(_dot_nt(ikb, iq_heads[h]), 0.0) * iw_ref[0, h:h + 1, :]
        bits = pltpu.bitcast(acc, jnp.int32)
        key = jnp.where(bits < 0, bits ^ 0x7FFFFFFF, bits)
        if masked:
            kpos = ks + koff
            key = jnp.where(kpos // CHUNK <= qpos // CHUNK, jnp.where(kpos < s_valid, key, KEY_NEG_INF),
                            KEY_NEG_INF)
        keys_ref[pl.ds(ks, kb), :] = key
        half_ref[pl.ds(ks, kb), :] = (key >> 16).astype(jnp.int16)
        return 0

    lax.fori_loop(0, nkb - 1, functools.partial(score_body, masked=False), 0)
    score_body(nkb - 1, 0, masked=True)

    def count_ge16(cand):
        cand = cand.astype(jnp.int16)

        def body(j, c16):
            ks = pl.multiple_of(j * kb, kb)
            hit = jnp.where(half_ref[pl.ds(ks, kb), :] >= cand, jnp.int16(1), jnp.int16(0))
            parts = [hit[i * 16:(i + 1) * 16] for i in range(kb // 16)]
            while len(parts) > 1:
                parts = [a + b for a, b in zip(parts[::2], parts[1::2])]
            return c16 + parts[0]
        c16 = lax.fori_loop(0, nkb, body, jnp.zeros((16, tq), jnp.int16))
        return jnp.sum(c16.astype(jnp.int32), axis=0, keepdims=True)

    def count_gt16(t):
        return jnp.where(t < I16_MAX, count_ge16(jnp.minimum(t + 1, I16_MAX)), 0)

    def kth_largest16(rank):
        def body(i, ub):
            c = ub | jnp.left_shift(jnp.int32(1), 15 - i)
            return jnp.where(count_ge16(c + I16_MIN) >= rank, c, ub)
        return lax.fori_loop(0, 16, body, jnp.zeros((1, tq), jnp.int32)) + I16_MIN

    t_hi = kth_largest16(k_sel)
    above = count_gt16(t_hi)

    def low_half_body(j, _):
        ks = pl.multiple_of(j * kb, kb)
        key = keys_ref[pl.ds(ks, kb), :]
        low = jnp.where((key >> 16) == t_hi, (key & 0xFFFF) + I16_MIN, I16_MIN)
        half_ref[pl.ds(ks, kb), :] = low.astype(jnp.int16)
        return 0

    lax.fori_loop(0, nkb, low_half_body, 0)
    t_lo = kth_largest16(k_sel - above)
    thr = t_hi * 65536 + (t_lo - I16_MIN)
    need = (k_sel - above - count_gt16(t_lo)).astype(F32)
    thr_next = thr + 1

    bq = bq_ref[0]
    blane = lax.broadcasted_iota(jnp.int32, (tq, DSA_W), 1)
    bq_heads = [jnp.where(blane // DSA_DIM == h, bq, jnp.zeros_like(bq)) for h in range(DSA_HEADS)]

    def att_body(j, carry, near):
        seen, state = carry
        ks = pl.multiple_of(j * kb, kb)
        key = keys_ref[pl.ds(ks, kb), :]
        tie = jnp.where(key == thr, 1.0, 0.0)
        rank = seen + _dot(low_ref[...], tie.astype(BF16))
        sel = key >= jnp.maximum(jnp.where(rank < need, thr, thr_next), KEY_NEG_INF + 1)
        bkb = bk_ref[0, pl.ds(ks, kb), :]
        vt = bvt_ref[0, :, pl.ds(ks, kb)]
        if near:
            bs = pl.multiple_of(ks - (q0 - BIAS_BACK), KB_SMALL)
        logits = []
        for h in range(DSA_HEADS):
            s = _dot_nt(bkb, bq_heads[h])
            s = s + (bias_ref[h, pl.ds(bs, kb), :] if near else far_ref[h])
            logits.append(jnp.where(sel, s, MASKED))
        return (seen + jnp.sum(tie, axis=0, keepdims=True),
                _softmax_steps(state, logits, [vt] * DSA_HEADS))

    carry = (jnp.zeros((1, tq), F32), _init_state(DSA_HEADS, tq))
    carry = lax.fori_loop(0, n_far, functools.partial(att_body, near=False), carry)
    carry = lax.fori_loop(n_far, nkb, functools.partial(att_body, near=True), carry)
    _finish(carry[1], o_ref)


def _dsa_call(far, iq, iw_t, ik, bq, bk, bv_t, bias_t, tq, past, s_valid, k_sel):
    b, t, _ = iq.shape
    s = ik.shape[1]
    low = jnp.asarray(np.tril(np.ones((KB_BIG, KB_BIG), np.float32), -1), BF16)
    return pl.pallas_call(
        functools.partial(_dsa_kernel, tq=tq, past=past, s_valid=s_valid, k_sel=k_sel),
        grid=(b, t // tq),
        in_specs=[
            pl.BlockSpec(memory_space=pltpu.SMEM),
            pl.BlockSpec((1, tq, IDX_W), lambda i, j: (i, j, 0)),
            pl.BlockSpec((1, 8, tq), lambda i, j: (i, 0, j)),
            pl.BlockSpec((1, s, IDX_W), lambda i, j: (i, 0, 0)),
            pl.BlockSpec((1, tq, DSA_W), lambda i, j: (i, j, 0)),
            pl.BlockSpec((1, s, DSA_W), lambda i, j: (i, 0, 0)),
            pl.BlockSpec((1, DSA_DIM, s), lambda i, j: (i, 0, 0)),
            _const_spec(bias_t.shape),
            _const_spec(low.shape),
        ],
        out_specs=pl.BlockSpec((1, tq, DSA_W), lambda i, j: (i, j, 0)),
        out_shape=jax.ShapeDtypeStruct((b, t, DSA_W), BF16),
        scratch_shapes=[pltpu.VMEM((s, tq), jnp.int32), pltpu.VMEM((s, tq), jnp.int16)],
        compiler_params=_params(("parallel", "parallel")),
        name="dsa",
    )(far, iq, iw_t, ik, bq, bk, bv_t, bias_t, low)


def _merge_kernel(x_ref, ya_ref, yb_ref, yc_ref, g_ref, wg_ref, wa_ref, wb_ref, wc_ref, wo_ref, o_ref):
    x = x_ref[...]
    d = x.shape[1]
    hb = _rms_rows(x, g_ref[...]).astype(BF16)
    mix = jnp.zeros_like(x)
    for i, (y_ref, w_ref) in enumerate(((ya_ref, wa_ref), (yb_ref, wb_ref), (yc_ref, wc_ref))):
        gate = jax.nn.sigmoid(_dot(hb, wg_ref[:, i * d:(i + 1) * d]))
        mix = mix + gate * _dot(y_ref[...], w_ref[...])
    o_ref[...] = x + _dot(mix.astype(BF16), wo_ref[...])


def _merge_call(x2d, ya, yb, yc, consts, tm):
    n, d = x2d.shape
    row = lambda w: pl.BlockSpec((tm, w), lambda i: (i, 0))
    return pl.pallas_call(
        _merge_kernel,
        grid=(n // tm,),
        in_specs=[row(d), row(FOX_W), row(DSA_W), row(MLA_W)] + [_const_spec(c.shape) for c in consts],
        out_specs=row(d),
        out_shape=jax.ShapeDtypeStruct((n, d), F32),
        compiler_params=_params(("parallel",)),
        name="merge",
    )(x2d, ya, yb, yc, *consts)


def _ffn_kernel(x_ref, g_ref, wi_ref, wo_ref, o_ref, *, n_chunks):
    x = x_ref[...]
    hb = _rms_rows(x, g_ref[...]).astype(BF16)
    out = x
    for c in range(n_chunks):
        gu = _dot(hb, wi_ref[c])
        half = gu.shape[1] // 2
        gt, up = gu[:, :half], gu[:, half:]
        out = out + _dot((gt * jax.nn.sigmoid(gt) * up).astype(BF16), wo_ref[c])
    o_ref[...] = out


def _ffn_call(x2d, g, wi, wo, tm):
    n, d = x2d.shape
    row = pl.BlockSpec((tm, d), lambda i: (i, 0))
    return pl.pallas_call(
        functools.partial(_ffn_kernel, n_chunks=wi.shape[0]),
        grid=(n // tm,),
        in_specs=[row, _const_spec(g.shape), _const_spec(wi.shape), _const_spec(wo.shape)],
        out_specs=row,
        out_shape=jax.ShapeDtypeStruct((n, d), F32),
        compiler_params=_params(("parallel",)),
        name="ffn",
    )(x2d, g, wi, wo)


def _block_avg(blocks, width):
    m = np.zeros((width, width), np.float32)
    for lo, hi in blocks:
        m[lo:hi, lo:hi] = 1.0 / (hi - lo)
    return jnp.asarray(m, BF16)


def _rot_matrix(base):
    r = np.zeros((LANE, LANE), np.float32)
    half = MLA_ROPE // 2
    for i in range(half):
        r[base + half + i, base + i] = -1.0
        r[base + i, base + half + i] = 1.0
    return jnp.asarray(r, BF16)


def _rope_tables(pos, base):
    half = MLA_ROPE // 2
    freq = ROPE_THETA ** (-jnp.arange(half, dtype=F32) / half)
    ang = pos.astype(F32)[:, None] * freq[None, :]
    cos, sin = jnp.cos(ang), jnp.sin(ang)
    n = pos.shape[0]
    ct = jnp.ones((n, LANE), F32).at[:, base:base + MLA_ROPE].set(jnp.concatenate([cos, cos], 1))
    st = jnp.zeros((n, LANE), F32).at[:, base:base + MLA_ROPE].set(jnp.concatenate([sin, sin], 1))
    return ct, st


def _t5_bucket(rel):
    nb = REL_BUCKETS // 2
    max_exact = nb // 2
    side = jnp.where(rel > 0, nb, 0)
    n = jnp.abs(rel)
    large = max_exact + (jnp.log(jnp.maximum(n, 1).astype(F32) / max_exact)
                         / math.log(REL_MAX_DIST / max_exact) * (nb - max_exact)).astype(jnp.int32)
    large = jnp.minimum(large, nb - 1)
    return side + jnp.where(n < max_exact, n, large)


def _bias_tables(rel_bias, tq):
    n_rel = BIAS_ROWS + tq - 1
    rel = jnp.arange(n_rel, dtype=jnp.int32) - (BIAS_BACK + tq - 1)
    per_rel = rel_bias[_t5_bucket(rel)].astype(F32).T * LOG2E
    flat = jnp.tile(per_rel, (1, BIAS_ROWS + 1))[:, :BIAS_ROWS * (n_rel + 1)]
    shifted = flat.reshape(DSA_HEADS, BIAS_ROWS, n_rel + 1)
    near = shifted[:, :, :tq][:, :, ::-1]
    far = rel_bias[_t5_bucket(jnp.int32(-REL_MAX_DIST))].astype(F32) * LOG2E
    return near, far


def _pad_cols(w, width):
    return jnp.pad(w, ((0, 0), (0, width - w.shape[1])))


def _tile_vec(g, reps, scale=1.0):
    return (jnp.tile(g.astype(F32), reps) * scale)[None, :]


def _layer_consts(p):
    splits = np.cumsum([FOX_W, FOX_W, FOX_W, FOX_HEADS, DSA_W, DSA_DIM, DSA_DIM, IDX_W, IDX_DIM, IDX_HEADS,
                        MLA_Q_LORA, MLA_KV_LORA, MLA_ROPE])[:-1]
    fq, fk, fv, fg, bq, bk, bv, iq, ik, iw, cqa, ckva, ckpe = jnp.split(p['w_in'], [int(v) for v in splits], axis=1)
    w_all = jnp.concatenate([
        fq, fk, fv, _pad_cols(fg, LANE), bq, jnp.tile(bk, (1, DSA_HEADS)), _pad_cols(bv, LANE),
        iq, jnp.tile(ik, (1, IDX_HEADS)), _pad_cols(iw, LANE), cqa, ckva, _pad_cols(ckpe, LANE)],
        axis=1).astype(BF16)
    assert w_all.shape[1] == _C_END
    d_qk = MLA_NOPE + MLA_ROPE
    wqb = p['mla_wqb'].reshape(MLA_Q_LORA, MLA_HEADS, d_qk)
    wqb = jnp.pad(wqb, ((0, 0), (0, 0), (0, LANE - d_qk))).reshape(MLA_Q_LORA, MLA_HEADS * LANE).astype(BF16)
    gqc = jnp.concatenate([p['mla_gqn'], p['mla_gqr'], jnp.zeros((LANE - d_qk,), F32)]).astype(F32)
    proj = [
        p['norm_mix'].astype(F32)[None, :], w_all,
        _pad_cols(p['fox_bf'].astype(F32)[None, :], LANE),
        _tile_vec(p['fox_gq'], FOX_HEADS, FOX_DIM ** -0.5 * LOG2E), _tile_vec(p['fox_gk'], FOX_HEADS),
        _tile_vec(p['dsa_gq'], DSA_HEADS, DSA_DIM ** -0.5 * LOG2E), _tile_vec(p['dsa_gk'], DSA_HEADS),
        _tile_vec(p['idx_gk'], IDX_HEADS),
        p['mla_gqa'].astype(F32)[None, :], p['mla_gkv'].astype(F32)[None, :],
        _pad_cols(p['mla_gkr'].astype(F32)[None, :], LANE),
        wqb, _tile_vec(gqc, MLA_HEADS, d_qk ** -0.5 * LOG2E),
        _block_avg([(i * 64, i * 64 + 64) for i in range(6)], FOX_W),
        _block_avg([(i * 64, i * 64 + 64) for i in range(4)], DSA_W),
        _block_avg([(i * 32, i * 32 + 32) for i in range(8)], IDX_W),
        _block_avg([(0, MLA_NOPE), (MLA_NOPE, d_qk)], LANE),
        _block_avg([(0, MLA_ROPE)], LANE),
        _rot_matrix(MLA_NOPE), _rot_matrix(0),
    ]
    wkvb = p['mla_wkvb'].reshape(MLA_KV_LORA, MLA_HEADS, MLA_NOPE + MLA_V)
    wk = jnp.pad(wkvb[:, :, :MLA_NOPE], ((0, 0), (0, 0), (0, LANE - MLA_NOPE)))
    wk = wk.reshape(MLA_KV_LORA, MLA_HEADS * LANE).astype(BF16)
    wv = wkvb[:, :, MLA_NOPE:].reshape(MLA_KV_LORA, MLA_W).astype(BF16)
    place = np.zeros((LANE, LANE), np.float32)
    place[np.arange(MLA_ROPE), MLA_NOPE + np.arange(MLA_ROPE)] = 1.0
    kvup = [wk, wv, _pad_cols(p['mla_gkn'].astype(F32)[None, :], LANE),
            _block_avg([(0, MLA_NOPE)], LANE), jnp.asarray(place, BF16)]
    merge = [p['norm_mix'].astype(F32)[None, :], p['w_gate'].astype(BF16), p['w_fox_out'].astype(BF16),
             p['w_dsa_out'].astype(BF16), p['w_mla_out'].astype(BF16), p['w_o'].astype(BF16)]
    d_ff = p['w_ffn_out'].shape[0]
    n_chunks = 2
    ck = d_ff // n_chunks
    wi = p['w_ffn_in']
    wi = jnp.stack([jnp.concatenate([wi[:, c * ck:(c + 1) * ck], wi[:, d_ff + c * ck:d_ff + (c + 1) * ck]], 1)
                    for c in range(n_chunks)]).astype(BF16)
    wo = p['w_ffn_out'].reshape(n_chunks, ck, -1).astype(BF16)
    ffn = [p['norm_ffn'].astype(F32)[None, :], wi, wo]
    return proj, kvup, merge, ffn


def _round_up(v, m):
    return -(-v // m) * m


def _layer(x, past, p, rel_bias, bias_cache):
    b, t, d = x.shape
    pl_len = 0 if past is None else past[0].shape[1]
    s_valid = pl_len + t
    n = b * t
    tm = min(512, n)
    t_att = _round_up(t, LANE)
    tq = min(256, t_att)
    assert pl_len % KB_SMALL == 0 and t_att % tq == 0
    s_pad = _round_up(pl_len + t_att - tq + max(tq, KB_SMALL), KB_BIG)
    k_sel = min(TOPK_MAX, s_valid // 4)
    proj_c, kvup_c, merge_c, ffn_c = _layer_consts(p)

    pos = pl_len + jnp.arange(t, dtype=jnp.int32)
    tabs = _rope_tables(pos, MLA_NOPE) + _rope_tables(pos, 0)
    if t < tm:
        tabs = tuple(jnp.tile(a, (tm // t, 1)) for a in tabs)
    n_tab_blocks = max(t // tm, 1)

    x2d = x.reshape(n, d)
    (fq, fk_f, fk_b, fv_f, fv_b, lf8, bq, bk_f, bk_r, bv_f, bv_b, iq, ik_f, ik_r, iw8, qc, ckv_f, kpe_f
     ) = _proj_call(x2d, proj_c, tabs, tm, n_tab_blocks)

    def seq(a):
        return a.reshape(b, t, a.shape[-1])

    def queries(a):
        return jnp.pad(seq(a), ((0, 0), (0, t_att - t), (0, 0)))

    def with_past(cached, new, dtype):
        new = seq(new).astype(dtype)
        if past is not None:
            new = jnp.concatenate([cached.reshape(b, pl_len, -1).astype(dtype), new], axis=1)
        return jnp.pad(new, ((0, 0), (0, s_pad - s_valid), (0, 0)))

    rows = (seq(fk_f).reshape(b, t, FOX_HEADS, FOX_DIM), seq(fv_f).reshape(b, t, FOX_HEADS, FOX_DIM),
            seq(lf8)[:, :, :FOX_HEADS], seq(bk_f), seq(bv_f), seq(ik_f), seq(ckv_f), seq(kpe_f))
    cache = (None,) * 8 if past is None else past

    lf_all = with_past(None if past is None else jnp.pad(cache[2], ((0, 0), (0, 0), (0, 2))), lf8, F32)
    cum_t = _cumsum_call(jnp.swapaxes(lf_all, 1, 2))
    ya = _fox_call(queries(fq), with_past(cache[0], fk_b, BF16),
                   jnp.swapaxes(with_past(cache[1], fv_b, BF16), 1, 2),
                   cum_t[:, :, pl_len:pl_len + t_att], jnp.swapaxes(cum_t, 1, 2), tq, pl_len)[:, :t]

    def rep(a, k):
        return jnp.tile(a.reshape(b, pl_len, -1), (1, 1, k))

    if tq not in bias_cache:
        bias_cache[tq] = _bias_tables(rel_bias, tq)
    bias_t, far = bias_cache[tq]
    ik_all = with_past(None if past is None else rep(cache[5], IDX_HEADS), ik_r, BF16)
    bk_all = with_past(None if past is None else rep(cache[3], DSA_HEADS), bk_r, BF16)
    bv_t = jnp.swapaxes(with_past(cache[4], bv_b, BF16), 1, 2)
    iw_t = jnp.swapaxes(queries(iw8), 1, 2)
    yb = _dsa_call(far, queries(iq), iw_t, ik_all, queries(bq), bk_all, bv_t, bias_t,
                   tq, pl_len, s_valid, k_sel)[:, :t]

    ckv_all = with_past(cache[6], ckv_f, F32).reshape(b * s_pad, MLA_KV_LORA)
    kpe_all = with_past(cache[7], kpe_f, BF16)
    kpe_all = jnp.pad(kpe_all, ((0, 0), (0, 0), (0, LANE - MLA_ROPE))).reshape(b * s_pad, LANE)
    kc, mv = _kvup_call(ckv_all, kpe_all, kvup_c, min(512, b * s_pad))
    yc = _mla_call(queries(qc), kc.reshape(b, s_pad, -1), jnp.swapaxes(mv.reshape(b, s_pad, -1), 1, 2),
                   tq, pl_len, s_valid)[:, :t]

    x1 = _merge_call(x2d, ya.reshape(n, -1), yb.reshape(n, -1), yc.reshape(n, -1), merge_c, tm)
    x2 = _ffn_call(x1, *ffn_c, tm)
    return x2.reshape(b, t, d), rows


def kernel(x_prompt, x_sample, cache_fox_k, cache_fox_v, cache_fox_logf, cache_dsa_k, cache_dsa_v, cache_idx_k, cache_mla_ckv, cache_mla_kpe, rel_bias, norm_mix, w_in, fox_gq, fox_gk, fox_bf, dsa_gq, dsa_gk, idx_gk, mla_gqa, mla_wqb, mla_gqn, mla_gqr, mla_gkv, mla_gkr, mla_wkvb, mla_gkn, w_fox_out, w_dsa_out, w_mla_out, w_gate, w_o, norm_ffn, w_ffn_in, w_ffn_out):
    caches = (cache_fox_k, cache_fox_v, cache_fox_logf, cache_dsa_k, cache_dsa_v,
              cache_idx_k, cache_mla_ckv, cache_mla_kpe)
    yp, ys = x_prompt, x_sample
    p_rows, s_rows, bias_cache = [], [], {}
    for i in range(norm_mix.shape[0]):
        p = dict(norm_mix=norm_mix[i], w_in=w_in[i], fox_gq=fox_gq[i], fox_gk=fox_gk[i],
                 fox_bf=fox_bf[i], dsa_gq=dsa_gq[i], dsa_gk=dsa_gk[i], idx_gk=idx_gk[i],
                 mla_gqa=mla_gqa[i], mla_wqb=mla_wqb[i], mla_gqn=mla_gqn[i], mla_gqr=mla_gqr[i],
                 mla_gkv=mla_gkv[i], mla_gkr=mla_gkr[i], mla_wkvb=mla_wkvb[i], mla_gkn=mla_gkn[i],
                 w_fox_out=w_fox_out[i], w_dsa_out=w_dsa_out[i], w_mla_out=w_mla_out[i],
                 w_gate=w_gate[i], w_o=w_o[i], norm_ffn=norm_ffn[i], w_ffn_in=w_ffn_in[i],
                 w_ffn_out=w_ffn_out[i])
        yp, rows_p = _layer(yp, None, p, rel_bias, bias_cache)
        ys, rows_s = _layer(ys, tuple(c[i] for c in caches), p, rel_bias, bias_cache)
        p_rows.append(rows_p)
        s_rows.append(rows_s)

    def st(rows, j):
        return jnp.stack([r[j] for r in rows], axis=0)

    return ((yp, ys) + tuple(st(p_rows, j) for j in range(8)) + tuple(st(s_rows, j) for j in range(8)))
```

```python
import functools
import math

import numpy as np
import jax
import jax.numpy as jnp
from jax import lax
from jax.experimental import pallas as pl
from jax.experimental.pallas import tpu as pltpu

F32 = jnp.float32
BF16 = jnp.bfloat16

CHUNK = 64
EPS = 1e-6
FOX_HEADS, FOX_DIM = 6, 64
FOX_W = FOX_HEADS * FOX_DIM
DSA_HEADS, DSA_DIM = 4, 64
DSA_W = DSA_HEADS * DSA_DIM
IDX_HEADS, IDX_DIM = 8, 32
IDX_W = IDX_HEADS * IDX_DIM
TOPK_MAX = 256
MLA_HEADS = 6
MLA_Q_LORA, MLA_KV_LORA = 256, 128
MLA_NOPE, MLA_ROPE, MLA_V = 64, 32, 64
MLA_W = MLA_HEADS * MLA_V
ROPE_THETA = 10000.0
REL_BUCKETS, REL_MAX_DIST = 32, 128

LANE = 128
KB_SMALL, KB_BIG = 256, 512
BIAS_BACK = 768
BIAS_ROWS = BIAS_BACK + KB_BIG
LOG2E = math.log2(math.e)
M_INIT = -1e30
MASKED = -3e30
INT_MIN = -2 ** 31
KEY_NEG_INF = -2139095041
I16_MIN, I16_MAX = -2 ** 15, 2 ** 15 - 1
VMEM_LIMIT = 56 * 1024 * 1024

_C_FQ, _C_FK, _C_FV, _C_FG = 0, 384, 768, 1152
_C_BQ, _C_BK, _C_BV = 1280, 1536, 1792
_C_IQ, _C_IK, _C_IW = 1920, 2176, 2432
_C_QA, _C_KV, _C_PE = 2560, 2816, 2944
_C_END = 3072


def _dot(a, b):
    return jnp.dot(a, b, preferred_element_type=F32)


def _dot_nt(a, b):
    return lax.dot_general(a, b, (((1,), (1,)), ((), ())), preferred_element_type=F32)


def _dot_hilo(x, m):
    hi = x.astype(BF16)
    lo = (x - hi.astype(F32)).astype(BF16)
    return _dot(hi, m) + _dot(lo, m)


def _rms_rows(x, g):
    return x * lax.rsqrt(jnp.mean(x * x, axis=-1, keepdims=True) + EPS) * g


def _rms_groups(x, avg, g):
    return x * lax.rsqrt(_dot((x * x).astype(BF16), avg) + EPS) * g


def _params(sem, vmem=VMEM_LIMIT):
    return pltpu.CompilerParams(dimension_semantics=sem, vmem_limit_bytes=vmem)


def _const_spec(shape):
    nd = len(shape)
    return pl.BlockSpec(shape, lambda *_: (0,) * nd)


def _proj_kernel(x_ref, gmix_ref, w_ref, bf_ref, gfq_ref, gfk_ref, gbq_ref, gbk_ref, gik_ref,
                 gqa_ref, gkv_ref, gkr_ref, wqb_ref, gqc_ref,
                 a64x6_ref, a64x4_ref, a32x8_ref, aqc_ref, ape_ref, rq_ref, rk_ref,
                 cosq_ref, sinq_ref, cosk_ref, sink_ref,
                 fq_o, fkf_o, fkb_o, fvf_o, fvb_o, lf_o, bq_o, bkf_o, bkr_o, bvf_o, bvb_o,
                 iq_o, ikf_o, ikr_o, iw_o, qc_o, ckv_o, kpe_o):
    x = x_ref[...]
    hb = _rms_rows(x, gmix_ref[...]).astype(BF16)

    def grp(lo, hi):
        return _dot(hb, w_ref[:, lo:hi])

    fq_o[...] = _rms_groups(grp(_C_FQ, _C_FK), a64x6_ref[...], gfq_ref[...]).astype(BF16)
    fk = _rms_groups(grp(_C_FK, _C_FV), a64x6_ref[...], gfk_ref[...])
    fkf_o[...] = fk
    fkb_o[...] = fk.astype(BF16)
    fv = grp(_C_FV, _C_FG)
    fvf_o[...] = fv
    fvb_o[...] = fv.astype(BF16)
    z = grp(_C_FG, _C_BQ) + bf_ref[...]
    lf = jnp.minimum(z, 0.0) - jnp.log1p(jnp.exp(-jnp.abs(z)))
    lf_o[...] = lf[:, :8]

    bq_o[...] = _rms_groups(grp(_C_BQ, _C_BK), a64x4_ref[...], gbq_ref[...]).astype(BF16)
    bk = _rms_groups(grp(_C_BK, _C_BV), a64x4_ref[...], gbk_ref[...])
    bkf_o[...] = bk[:, :DSA_DIM]
    bkr_o[...] = bk.astype(BF16)
    bv = grp(_C_BV, _C_IQ)[:, :DSA_DIM]
    bvf_o[...] = bv
    bvb_o[...] = bv.astype(BF16)
    iq_o[...] = grp(_C_IQ, _C_IK).astype(BF16)
    ik = _rms_groups(grp(_C_IK, _C_IW), a32x8_ref[...], gik_ref[...])
    ikf_o[...] = ik[:, :IDX_DIM]
    ikr_o[...] = ik.astype(BF16)
    iw_o[...] = (grp(_C_IW, _C_QA) * (1.0 / 16.0))[:, :8]

    cq = _rms_rows(grp(_C_QA, _C_KV), gqa_ref[...]).astype(BF16)
    cosq, sinq = cosq_ref[...], sinq_ref[...]
    for h in range(MLA_HEADS):
        sl = slice(h * LANE, (h + 1) * LANE)
        qh = _rms_groups(_dot(cq, wqb_ref[:, sl]), aqc_ref[...], gqc_ref[:, sl])
        qc_o[:, sl] = (qh * cosq + _dot(qh.astype(BF16), rq_ref[...]) * sinq).astype(BF16)
    ckv_o[...] = _rms_rows(grp(_C_KV, _C_PE), gkv_ref[...])
    kp = _rms_groups(grp(_C_PE, _C_END), ape_ref[...], gkr_ref[...])
    kp = kp * cosk_ref[...] + _dot_hilo(kp, rk_ref[...]) * sink_ref[...]
    kpe_o[...] = kp[:, :MLA_ROPE]


def _proj_call(x2d, consts, tabs, tm, n_tab_blocks):
    n = x2d.shape[0]
    row = lambda w: pl.BlockSpec((tm, w), lambda i: (i, 0))
    tab = pl.BlockSpec((tm, LANE), lambda i: (i % n_tab_blocks, 0))
    in_specs = [row(x2d.shape[1])] + [_const_spec(c.shape) for c in consts] + [tab] * 4
    widths = [(FOX_W, BF16), (FOX_W, F32), (FOX_W, BF16), (FOX_W, F32), (FOX_W, BF16), (8, F32),
              (DSA_W, BF16), (DSA_DIM, F32), (DSA_W, BF16), (DSA_DIM, F32), (DSA_DIM, BF16),
              (IDX_W, BF16), (IDX_DIM, F32), (IDX_W, BF16), (8, F32),
              (MLA_HEADS * LANE, BF16), (MLA_KV_LORA, F32), (MLA_ROPE, F32)]
    return pl.pallas_call(
        _proj_kernel,
        grid=(n // tm,),
        in_specs=in_specs,
        out_specs=[row(w) for w, _ in widths],
        out_shape=[jax.ShapeDtypeStruct((n, w), d) for w, d in widths],
        compiler_params=_params(("parallel",)),
        name="proj",
    )(x2d, *consts, *tabs)


def _cumsum_kernel(x_ref, tri_ref, o_ref):
    nblk = x_ref.shape[2] // LANE

    def body(j, carry):
        s = pl.multiple_of(j * LANE, LANE)
        x = x_ref[0, :, pl.ds(s, LANE)]
        x1 = x.astype(BF16)
        r1 = x - x1.astype(F32)
        x2 = r1.astype(BF16)
        x3 = (r1 - x2.astype(F32)).astype(BF16)
        tri = tri_ref[...]
        out = _dot(x1, tri) + _dot(x2, tri) + _dot(x3, tri) + carry
        o_ref[0, :, pl.ds(s, LANE)] = out * LOG2E
        return out[:, LANE - 1:LANE]

    lax.fori_loop(0, nblk, body, jnp.zeros((8, 1), F32))


def _cumsum_call(lf_t):
    b, _, s = lf_t.shape
    tri = jnp.asarray(np.triu(np.ones((LANE, LANE), np.float32)), BF16)
    return pl.pallas_call(
        _cumsum_kernel,
        grid=(b,),
        in_specs=[pl.BlockSpec((1, 8, s), lambda i: (i, 0, 0)), _const_spec((LANE, LANE))],
        out_specs=pl.BlockSpec((1, 8, s), lambda i: (i, 0, 0)),
        out_shape=jax.ShapeDtypeStruct((b, 8, s), F32),
        compiler_params=_params(("parallel",)),
        name="cumsum",
    )(lf_t, tri)


def _softmax_steps(states, logits, vts):
    mid = []
    for (m, l, acc), s in zip(states, logits):
        m_new = jnp.maximum(m, jnp.max(s, axis=0, keepdims=True))
        alpha = jnp.exp2(m - m_new)
        p = jnp.exp2(s - m_new)
        mid.append((m_new, alpha * l + jnp.sum(p, axis=0, keepdims=True), alpha * acc, p.astype(BF16)))
    return tuple((m_new, l, acc + _dot(vt, p)) for (m_new, l, acc, p), vt in zip(mid, vts))


def _init_state(n_heads, tq):
    return tuple((jnp.full((1, tq), M_INIT, F32), jnp.zeros((1, tq), F32), jnp.zeros((64, tq), F32))
                 for _ in range(n_heads))


def _finish(state, o_ref):
    out_t = jnp.concatenate([acc / l for _, l, acc in state], axis=0)
    o_ref[0] = out_t.T.astype(o_ref.dtype)


def _sweep(n_small, step, carry):
    ratio = KB_BIG // KB_SMALL
    n_big = n_small // ratio

    def big(j, c):
        return step(pl.multiple_of(j * KB_BIG, KB_BIG), KB_BIG, c)

    def small(j, c):
        return step(pl.multiple_of(j * KB_SMALL, KB_SMALL), KB_SMALL, c)

    carry = lax.fori_loop(0, n_big, big, carry)
    return lax.fori_loop(n_big * ratio, n_small, small, carry)


def _fox_kernel(q_ref, k_ref, vt_ref, cq_ref, ck_ref, o_ref, *, tq, past):
    q0 = past + pl.program_id(1) * tq
    lane = lax.broadcasted_iota(jnp.int32, (tq, LANE), 1)
    q_heads, cq = [], []
    for h in range(FOX_HEADS):
        qp = q_ref[0, :, (h // 2) * LANE:(h // 2 + 1) * LANE]
        q_heads.append(jnp.where((lane < 64) == (h % 2 == 0), qp, jnp.zeros_like(qp)))
        cq.append(cq_ref[0, h:h + 1, :])

    def step(ks, width, state, masked=False):
        if masked:
            ok = (ks + lax.broadcasted_iota(jnp.int32, (width, tq), 0)
                  <= q0 + lax.broadcasted_iota(jnp.int32, (width, tq), 1))
        logits = []
        for h in range(FOX_HEADS):
            kb = k_ref[0, pl.ds(ks, width), (h // 2) * LANE:(h // 2 + 1) * LANE]
            s = _dot_nt(kb, q_heads[h]) + cq[h] - ck_ref[0, pl.ds(ks, width), h:h + 1]
            logits.append(jnp.where(ok, s, MASKED) if masked else s)
        vts = [vt_ref[0, h * 64:(h + 1) * 64, pl.ds(ks, width)] for h in range(FOX_HEADS)]
        return _softmax_steps(state, logits, vts)

    n_free = q0 // KB_SMALL
    state = _sweep(n_free, step, _init_state(FOX_HEADS, tq))
    state = step(pl.multiple_of(n_free * KB_SMALL, KB_SMALL), KB_SMALL, state, masked=True)
    _finish(state, o_ref)


def _fox_call(q, k, vt, cq, ck, tq, past):
    b, t, _ = q.shape
    s = k.shape[1]
    return pl.pallas_call(
        functools.partial(_fox_kernel, tq=tq, past=past),
        grid=(b, t // tq),
        in_specs=[
            pl.BlockSpec((1, tq, FOX_W), lambda i, j: (i, j, 0)),
            pl.BlockSpec((1, s, FOX_W), lambda i, j: (i, 0, 0)),
            pl.BlockSpec((1, FOX_W, s), lambda i, j: (i, 0, 0)),
            pl.BlockSpec((1, 8, tq), lambda i, j: (i, 0, j)),
            pl.BlockSpec((1, s, 8), lambda i, j: (i, 0, 0)),
        ],
        out_specs=pl.BlockSpec((1, tq, FOX_W), lambda i, j: (i, j, 0)),
        out_shape=jax.ShapeDtypeStruct((b, t, FOX_W), BF16),
        compiler_params=_params(("parallel", "parallel")),
        name="fox",
    )(q, k, vt, cq, ck)


def _mla_kernel(q_ref, k_ref, vt_ref, o_ref, *, tq, past, s_valid):
    q0 = past + pl.program_id(1) * tq
    q_heads = [q_ref[0, :, h * LANE:(h + 1) * LANE] for h in range(MLA_HEADS)]

    def step(ks, width, state, masked=False):
        if masked:
            kpos = ks + lax.broadcasted_iota(jnp.int32, (width, tq), 0)
            qchunk = (q0 + lax.broadcasted_iota(jnp.int32, (width, tq), 1)) // CHUNK
            ok = (kpos // CHUNK <= qchunk) & (kpos < s_valid)
        logits = []
        for h in range(MLA_HEADS):
            s = _dot_nt(k_ref[0, pl.ds(ks, width), h * LANE:(h + 1) * LANE], q_heads[h])
            logits.append(jnp.where(ok, s, MASKED) if masked else s)
        vts = [vt_ref[0, h * 64:(h + 1) * 64, pl.ds(ks, width)] for h in range(MLA_HEADS)]
        return _softmax_steps(state, logits, vts)

    n_free = q0 // KB_SMALL
    state = _sweep(n_free, step, _init_state(MLA_HEADS, tq))
    state = step(pl.multiple_of(n_free * KB_SMALL, KB_SMALL), KB_SMALL, state, masked=True)
    _finish(state, o_ref)


def _mla_call(q, k, vt, tq, past, s_valid):
    b, t, _ = q.shape
    s = k.shape[1]
    return pl.pallas_call(
        functools.partial(_mla_kernel, tq=tq, past=past, s_valid=s_valid),
        grid=(b, t // tq),
        in_specs=[
            pl.BlockSpec((1, tq, MLA_HEADS * LANE), lambda i, j: (i, j, 0)),
            pl.BlockSpec((1, s, MLA_HEADS * LANE), lambda i, j: (i, 0, 0)),
            pl.BlockSpec((1, MLA_W, s), lambda i, j: (i, 0, 0)),
        ],
        out_specs=pl.BlockSpec((1, tq, MLA_W), lambda i, j: (i, j, 0)),
        out_shape=jax.ShapeDtypeStruct((b, t, MLA_W), BF16),
        compiler_params=_params(("parallel", "parallel")),
        name="mla",
    )(q, k, vt)


def _kvup_kernel(ckv_ref, kpe_ref, wk_ref, wv_ref, gkn_ref, akn_ref, place_ref, k_o, v_o):
    cb = ckv_ref[...].astype(BF16)
    v_o[...] = _dot(cb, wv_ref[...]).astype(BF16)
    pe = _dot(kpe_ref[...], place_ref[...])
    for h in range(MLA_HEADS):
        sl = slice(h * LANE, (h + 1) * LANE)
        kn = _rms_groups(_dot(cb, wk_ref[:, sl]), akn_ref[...], gkn_ref[...])
        k_o[:, sl] = (kn + pe).astype(BF16)


def _kvup_call(ckv2d, kpe2d, consts, tm):
    n = ckv2d.shape[0]
    row = lambda w: pl.BlockSpec((tm, w), lambda i: (i, 0))
    return pl.pallas_call(
        _kvup_kernel,
        grid=(n // tm,),
        in_specs=[row(MLA_KV_LORA), row(LANE)] + [_const_spec(c.shape) for c in consts],
        out_specs=[row(MLA_HEADS * LANE), row(MLA_W)],
        out_shape=[jax.ShapeDtypeStruct((n, MLA_HEADS * LANE), BF16),
                   jax.ShapeDtypeStruct((n, MLA_W), BF16)],
        compiler_params=_params(("parallel",)),
        name="kvup",
    )(ckv2d, kpe2d, *consts)


def _dsa_kernel(far_ref, iq_ref, iw_ref, ik_ref, bq_ref, bk_ref, bvt_ref, bias_ref, low_ref,
                o_ref, keys_ref, half_ref, *, tq, past, s_valid, k_sel):
    kb = KB_BIG
    q0 = past + pl.program_id(1) * tq
    nkb = (jnp.minimum(q0 + tq, s_valid) + kb - 1) // kb
    n_far = jnp.maximum(nkb - 2, 0)
    qpos = q0 + lax.broadcasted_iota(jnp.int32, (kb, tq), 1)
    koff = lax.broadcasted_iota(jnp.int32, (kb, tq), 0)
    qlane = lax.broadcasted_iota(jnp.int32, (tq, IDX_W), 1)

    iq = iq_ref[0]
    iq_heads = [jnp.where(qlane // IDX_DIM == h, iq, jnp.zeros_like(iq)) for h in range(IDX_HEADS)]

    def score_body(j, _, masked):
        ks = pl.multiple_of(j * kb, kb)
        ikb = ik_ref[0, pl.ds(ks, kb), :]
        acc = jnp.zeros((kb, tq), F32)
        for h in range(IDX_HEADS):
            acc = acc + jnp.maximum(_dot_nt(ikb, iq_heads[h]), 0.0) * iw_ref[0, h:h + 1, :]
        bits = pltpu.bitcast(acc, jnp.int32)
        key = jnp.where(bits < 0, bits ^ 0x7FFFFFFF, bits)
        if masked:
            kpos = ks + koff
            key = jnp.where(kpos // CHUNK <= qpos // CHUNK, jnp.where(kpos < s_valid, key, KEY_NEG_INF),
                            KEY_NEG_INF)
        keys_ref[pl.ds(ks, kb), :] = key
        half_ref[pl.ds(ks, kb), :] = (key >> 16).astype(jnp.int16)
        return 0

    lax.fori_loop(0, nkb - 1, functools.partial(score_body, masked=False), 0)
    score_body(nkb - 1, 0, masked=True)

    def count_ge16(cand):
        cand = cand.astype(jnp.int16)

        def body(j, c16):
            ks = pl.multiple_of(j * kb, kb)
            hit = jnp.where(half_ref[pl.ds(ks, kb), :] >= cand, jnp.int16(1), jnp.int16(0))
            parts = [hit[i * 16:(i + 1) * 16] for i in range(kb // 16)]
            while len(parts) > 1:
                parts = [a + b for a, b in zip(parts[::2], parts[1::2])]
            return c16 + parts[0]
        c16 = lax.fori_loop(0, nkb, body, jnp.zeros((16, tq), jnp.int16))
        return jnp.sum(c16.astype(jnp.int32), axis=0, keepdims=True)

    def count_gt16(t):
        return jnp.where(t < I16_MAX, count_ge16(jnp.minimum(t + 1, I16_MAX)), 0)

    def kth_largest16(rank):
        def body(i, ub):
            c = ub | jnp.left_shift(jnp.int32(1), 15 - i)
            return jnp.where(count_ge16(c + I16_MIN) >= rank, c, ub)
        return lax.fori_loop(0, 16, body, jnp.zeros((1, tq), jnp.int32)) + I16_MIN

    t_hi = kth_largest16(k_sel)
    above = count_gt16(t_hi)

    def low_half_body(j, _):
        ks = pl.multiple_of(j * kb, kb)
        key = keys_ref[pl.ds(ks, kb), :]
        low = jnp.where((key >> 16) == t_hi, (key & 0xFFFF) + I16_MIN, I16_MIN)
        half_ref[pl.ds(ks, kb), :] = low.astype(jnp.int16)
        return 0

    lax.fori_loop(0, nkb, low_half_body, 0)
    t_lo = kth_largest16(k_sel - above)
    thr = t_hi * 65536 + (t_lo - I16_MIN)
    need = (k_sel - above - count_gt16(t_lo)).astype(F32)
    thr_next = thr + 1

    bq = bq_ref[0]
    blane = lax.broadcasted_iota(jnp.int32, (tq, DSA_W), 1)
    bq_heads = [jnp.where(blane // DSA_DIM == h, bq, jnp.zeros_like(bq)) for h in range(DSA_HEADS)]

    def att_body(j, carry, near):
        seen, state = carry
        ks = pl.multiple_of(j * kb, kb)
        key = keys_ref[pl.ds(ks, kb), :]
        tie = jnp.where(key == thr, 1.0, 0.0)
        rank = seen + _dot(low_ref[...], tie.astype(BF16))
        sel = key >= jnp.maximum(jnp.where(rank < need, thr, thr_next), KEY_NEG_INF + 1)
        bkb = bk_ref[0, pl.ds(ks, kb), :]
        vt = bvt_ref[0, :, pl.ds(ks, kb)]
        if near:
            bs = pl.multiple_of(ks - (q0 - BIAS_BACK), KB_SMALL)
        logits = []
        for h in range(DSA_HEADS):
            s = _dot_nt(bkb, bq_heads[h])
            s = s + (bias_ref[h, pl.ds(bs, kb), :] if near else far_ref[h])
            logits.append(jnp.where(sel, s, MASKED))
        return (seen + jnp.sum(tie, axis=0, keepdims=True),
                _softmax_steps(state, logits, [vt] * DSA_HEADS))

    carry = (jnp.zeros((1, tq), F32), _init_state(DSA_HEADS, tq))
    carry = lax.fori_loop(0, n_far, functools.partial(att_body, near=False), carry)
    carry = lax.fori_loop(n_far, nkb, functools.partial(att_body, near=True), carry)
    _finish(carry[1], o_ref)


def _dsa_call(far, iq, iw_t, ik, bq, bk, bv_t, bias_t, tq, past, s_valid, k_sel):
    b, t, _ = iq.shape
    s = ik.shape[1]
    low = jnp.asarray(np.tril(np.ones((KB_BIG, KB_BIG), np.float32), -1), BF16)
    return pl.pallas_call(
        functools.partial(_dsa_kernel, tq=tq, past=past, s_valid=s_valid, k_sel=k_sel),
        grid=(b, t // tq),
        in_specs=[
            pl.BlockSpec(memory_space=pltpu.SMEM),
            pl.BlockSpec((1, tq, IDX_W), lambda i, j: (i, j, 0)),
            pl.BlockSpec((1, 8, tq), lambda i, j: (i, 0, j)),
            pl.BlockSpec((1, s, IDX_W), lambda i, j: (i, 0, 0)),
            pl.BlockSpec((1, tq, DSA_W), lambda i, j: (i, j, 0)),
            pl.BlockSpec((1, s, DSA_W), lambda i, j: (i, 0, 0)),
            pl.BlockSpec((1, DSA_DIM, s), lambda i, j: (i, 0, 0)),
            _const_spec(bias_t.shape),
            _const_spec(low.shape),
        ],
        out_specs=pl.BlockSpec((1, tq, DSA_W), lambda i, j: (i, j, 0)),
        out_shape=jax.ShapeDtypeStruct((b, t, DSA_W), BF16),
        scratch_shapes=[pltpu.VMEM((s, tq), jnp.int32), pltpu.VMEM((s, tq), jnp.int16)],
        compiler_params=_params(("parallel", "parallel")),
        name="dsa",
    )(far, iq, iw_t, ik, bq, bk, bv_t, bias_t, low)


def _merge_kernel(x_ref, ya_ref, yb_ref, yc_ref, g_ref, wg_ref, wa_ref, wb_ref, wc_ref, wo_ref, o_ref):
    x = x_ref[...]
    d = x.shape[1]
    hb = _rms_rows(x, g_ref[...]).astype(BF16)
    mix = jnp.zeros_like(x)
    for i, (y_ref, w_ref) in enumerate(((ya_ref, wa_ref), (yb_ref, wb_ref), (yc_ref, wc_ref))):
        gate = jax.nn.sigmoid(_dot(hb, wg_ref[:, i * d:(i + 1) * d]))
        mix = mix + gate * _dot(y_ref[...], w_ref[...])
    o_ref[...] = x + _dot(mix.astype(BF16), wo_ref[...])


def _merge_call(x2d, ya, yb, yc, consts, tm):
    n, d = x2d.shape
    row = lambda w: pl.BlockSpec((tm, w), lambda i: (i, 0))
    return pl.pallas_call(
        _merge_kernel,
        grid=(n // tm,),
        in_specs=[row(d), row(FOX_W), row(DSA_W), row(MLA_W)] + [_const_spec(c.shape) for c in consts],
        out_specs=row(d),
        out_shape=jax.ShapeDtypeStruct((n, d), F32),
        compiler_params=_params(("parallel",)),
        name="merge",
    )(x2d, ya, yb, yc, *consts)


def _ffn_kernel(x_ref, g_ref, wi_ref, wo_ref, o_ref, *, n_chunks):
    x = x_ref[...]
    hb = _rms_rows(x, g_ref[...]).astype(BF16)
    out = x
    for c in range(n_chunks):
        gu = _dot(hb, wi_ref[c])
        half = gu.shape[1] // 2
        gt, up = gu[:, :half], gu[:, half:]
        out = out + _dot((gt * jax.nn.sigmoid(gt) * up).astype(BF16), wo_ref[c])
    o_ref[...] = out


def _ffn_call(x2d, g, wi, wo, tm):
    n, d = x2d.shape
    row = pl.BlockSpec((tm, d), lambda i: (i, 0))
    return pl.pallas_call(
        functools.partial(_ffn_kernel, n_chunks=wi.shape[0]),
        grid=(n // tm,),
        in_specs=[row, _const_spec(g.shape), _const_spec(wi.shape), _const_spec(wo.shape)],
        out_specs=row,
        out_shape=jax.ShapeDtypeStruct((n, d), F32),
        compiler_params=_params(("parallel",)),
        name="ffn",
    )(x2d, g, wi, wo)


def _block_avg(blocks, width):
    m = np.zeros((width, width), np.float32)
    for lo, hi in blocks:
        m[lo:hi, lo:hi] = 1.0 / (hi - lo)
    return jnp.asarray(m, BF16)


def _rot_matrix(base):
    r = np.zeros((LANE, LANE), np.float32)
    half = MLA_ROPE // 2
    for i in range(half):
        r[base + half + i, base + i] = -1.0
        r[base + i, base + half + i] = 1.0
    return jnp.asarray(r, BF16)


def _rope_tables(pos, base):
    half = MLA_ROPE // 2
    freq = ROPE_THETA ** (-jnp.arange(half, dtype=F32) / half)
    ang = pos.astype(F32)[:, None] * freq[None, :]
    cos, sin = jnp.cos(ang), jnp.sin(ang)
    n = pos.shape[0]
    ct = jnp.ones((n, LANE), F32).at[:, base:base + MLA_ROPE].set(jnp.concatenate([cos, cos], 1))
    st = jnp.zeros((n, LANE), F32).at[:, base:base + MLA_ROPE].set(jnp.concatenate([sin, sin], 1))
    return ct, st


def _t5_bucket(rel):
    nb = REL_BUCKETS // 2
    max_exact = nb // 2
    side = jnp.where(rel > 0, nb, 0)
    n = jnp.abs(rel)
    large = max_exact + (jnp.log(jnp.maximum(n, 1).astype(F32) / max_exact)
                         / math.log(REL_MAX_DIST / max_exact) * (nb - max_exact)).astype(jnp.int32)
    large = jnp.minimum(large, nb - 1)
    return side + jnp.where(n < max_exact, n, large)


def _bias_tables(rel_bias, tq):
    c = jnp.arange(BIAS_ROWS, dtype=jnp.int32)[:, None]
    r = jnp.arange(tq, dtype=jnp.int32)[None, :]
    onehot = (_t5_bucket(c - BIAS_BACK - r)[None] == jnp.arange(REL_BUCKETS)[:, None, None]).astype(F32)
    near = jnp.einsum('bcr,bh->hcr', onehot, rel_bias.astype(F32), precision=lax.Precision.HIGHEST) * LOG2E
    far = rel_bias[_t5_bucket(jnp.int32(-REL_MAX_DIST))].astype(F32) * LOG2E
    return near, far


def _pad_cols(w, width):
    return jnp.pad(w, ((0, 0), (0, width - w.shape[1])))


def _tile_vec(g, reps, scale=1.0):
    return (jnp.tile(g.astype(F32), reps) * scale)[None, :]


def _layer_consts(p):
    splits = np.cumsum([FOX_W, FOX_W, FOX_W, FOX_HEADS, DSA_W, DSA_DIM, DSA_DIM, IDX_W, IDX_DIM, IDX_HEADS,
                        MLA_Q_LORA, MLA_KV_LORA, MLA_ROPE])[:-1]
    fq, fk, fv, fg, bq, bk, bv, iq, ik, iw, cqa, ckva, ckpe = jnp.split(p['w_in'], [int(v) for v in splits], axis=1)
    w_all = jnp.concatenate([
        fq, fk, fv, _pad_cols(fg, LANE), bq, jnp.tile(bk, (1, DSA_HEADS)), _pad_cols(bv, LANE),
        iq, jnp.tile(ik, (1, IDX_HEADS)), _pad_cols(iw, LANE), cqa, ckva, _pad_cols(ckpe, LANE)],
        axis=1).astype(BF16)
    assert w_all.shape[1] == _C_END
    d_qk = MLA_NOPE + MLA_ROPE
    wqb = p['mla_wqb'].reshape(MLA_Q_LORA, MLA_HEADS, d_qk)
    wqb = jnp.pad(wqb, ((0, 0), (0, 0), (0, LANE - d_qk))).reshape(MLA_Q_LORA, MLA_HEADS * LANE).astype(BF16)
    gqc = jnp.concatenate([p['mla_gqn'], p['mla_gqr'], jnp.zeros((LANE - d_qk,), F32)]).astype(F32)
    proj = [
        p['norm_mix'].astype(F32)[None, :], w_all,
        _pad_cols(p['fox_bf'].astype(F32)[None, :], LANE),
        _tile_vec(p['fox_gq'], FOX_HEADS, FOX_DIM ** -0.5 * LOG2E), _tile_vec(p['fox_gk'], FOX_HEADS),
        _tile_vec(p['dsa_gq'], DSA_HEADS, DSA_DIM ** -0.5 * LOG2E), _tile_vec(p['dsa_gk'], DSA_HEADS),
        _tile_vec(p['idx_gk'], IDX_HEADS),
        p['mla_gqa'].astype(F32)[None, :], p['mla_gkv'].astype(F32)[None, :],
        _pad_cols(p['mla_gkr'].astype(F32)[None, :], LANE),
        wqb, _tile_vec(gqc, MLA_HEADS, d_qk ** -0.5 * LOG2E),
        _block_avg([(i * 64, i * 64 + 64) for i in range(6)], FOX_W),
        _block_avg([(i * 64, i * 64 + 64) for i in range(4)], DSA_W),
        _block_avg([(i * 32, i * 32 + 32) for i in range(8)], IDX_W),
        _block_avg([(0, MLA_NOPE), (MLA_NOPE, d_qk)], LANE),
        _block_avg([(0, MLA_ROPE)], LANE),
        _rot_matrix(MLA_NOPE), _rot_matrix(0),
    ]
    wkvb = p['mla_wkvb'].reshape(MLA_KV_LORA, MLA_HEADS, MLA_NOPE + MLA_V)
    wk = jnp.pad(wkvb[:, :, :MLA_NOPE], ((0, 0), (0, 0), (0, LANE - MLA_NOPE)))
    wk = wk.reshape(MLA_KV_LORA, MLA_HEADS * LANE).astype(BF16)
    wv = wkvb[:, :, MLA_NOPE:].reshape(MLA_KV_LORA, MLA_W).astype(BF16)
    place = np.zeros((LANE, LANE), np.float32)
    place[np.arange(MLA_ROPE), MLA_NOPE + np.arange(MLA_ROPE)] = 1.0
    kvup = [wk, wv, _pad_cols(p['mla_gkn'].astype(F32)[None, :], LANE),
            _block_avg([(0, MLA_NOPE)], LANE), jnp.asarray(place, BF16)]
    merge = [p['norm_mix'].astype(F32)[None, :], p['w_gate'].astype(BF16), p['w_fox_out'].astype(BF16),
             p['w_dsa_out'].astype(BF16), p['w_mla_out'].astype(BF16), p['w_o'].astype(BF16)]
    d_ff = p['w_ffn_out'].shape[0]
    n_chunks = 2
    ck = d_ff // n_chunks
    wi = p['w_ffn_in']
    wi = jnp.stack([jnp.concatenate([wi[:, c * ck:(c + 1) * ck], wi[:, d_ff + c * ck:d_ff + (c + 1) * ck]], 1)
                    for c in range(n_chunks)]).astype(BF16)
    wo = p['w_ffn_out'].reshape(n_chunks, ck, -1).astype(BF16)
    ffn = [p['norm_ffn'].astype(F32)[None, :], wi, wo]
    return proj, kvup, merge, ffn


def _round_up(v, m):
    return -(-v // m) * m


def _layer(x, past, p, rel_bias, bias_cache):
    b, t, d = x.shape
    pl_len = 0 if past is None else past[0].shape[1]
    s_valid = pl_len + t
    n = b * t
    tm = min(512, n)
    t_att = _round_up(t, LANE)
    tq = min(256, t_att)
    assert pl_len % KB_SMALL == 0 and t_att % tq == 0
    s_pad = _round_up(pl_len + t_att - tq + max(tq, KB_SMALL), KB_BIG)
    k_sel = min(TOPK_MAX, s_valid // 4)
    proj_c, kvup_c, merge_c, ffn_c = _layer_consts(p)

    pos = pl_len + jnp.arange(t, dtype=jnp.int32)
    tabs = _rope_tables(pos, MLA_NOPE) + _rope_tables(pos, 0)
    if t < tm:
        tabs = tuple(jnp.tile(a, (tm // t, 1)) for a in tabs)
    n_tab_blocks = max(t // tm, 1)

    x2d = x.reshape(n, d)
    (fq, fk_f, fk_b, fv_f, fv_b, lf8, bq, bk_f, bk_r, bv_f, bv_b, iq, ik_f, ik_r, iw8, qc, ckv_f, kpe_f
     ) = _proj_call(x2d, proj_c, tabs, tm, n_tab_blocks)

    def seq(a):
        return a.reshape(b, t, a.shape[-1])

    def queries(a):
        return jnp.pad(seq(a), ((0, 0), (0, t_att - t), (0, 0)))

    def with_past(cached, new, dtype):
        new = seq(new).astype(dtype)
        if past is not None:
            new = jnp.concatenate([cached.reshape(b, pl_len, -1).astype(dtype), new], axis=1)
        return jnp.pad(new, ((0, 0), (0, s_pad - s_valid), (0, 0)))

    rows = (seq(fk_f).reshape(b, t, FOX_HEADS, FOX_DIM), seq(fv_f).reshape(b, t, FOX_HEADS, FOX_DIM),
            seq(lf8)[:, :, :FOX_HEADS], seq(bk_f), seq(bv_f), seq(ik_f), seq(ckv_f), seq(kpe_f))
    cache = (None,) * 8 if past is None else past

    lf_all = with_past(None if past is None else jnp.pad(cache[2], ((0, 0), (0, 0), (0, 2))), lf8, F32)
    cum_t = _cumsum_call(jnp.swapaxes(lf_all, 1, 2))
    ya = _fox_call(queries(fq), with_past(cache[0], fk_b, BF16),
                   jnp.swapaxes(with_past(cache[1], fv_b, BF16), 1, 2),
                   cum_t[:, :, pl_len:pl_len + t_att], jnp.swapaxes(cum_t, 1, 2), tq, pl_len)[:, :t]

    def rep(a, k):
        return jnp.tile(a.reshape(b, pl_len, -1), (1, 1, k))

    if tq not in bias_cache:
        bias_cache[tq] = _bias_tables(rel_bias, tq)
    bias_t, far = bias_cache[tq]
    ik_all = with_past(None if past is None else rep(cache[5], IDX_HEADS), ik_r, BF16)
    bk_all = with_past(None if past is None else rep(cache[3], DSA_HEADS), bk_r, BF16)
    bv_t = jnp.swapaxes(with_past(cache[4], bv_b, BF16), 1, 2)
    iw_t = jnp.swapaxes(queries(iw8), 1, 2)
    yb = _dsa_call(far, queries(iq), iw_t, ik_all, queries(bq), bk_all, bv_t, bias_t,
                   tq, pl_len, s_valid, k_sel)[:, :t]

    ckv_all = with_past(cache[6], ckv_f, F32).reshape(b * s_pad, MLA_KV_LORA)
    kpe_all = with_past(cache[7], kpe_f, BF16)
    kpe_all = jnp.pad(kpe_all, ((0, 0), (0, 0), (0, LANE - MLA_ROPE))).reshape(b * s_pad, LANE)
    kc, mv = _kvup_call(ckv_all, kpe_all, kvup_c, min(512, b * s_pad))
    yc = _mla_call(queries(qc), kc.reshape(b, s_pad, -1), jnp.swapaxes(mv.reshape(b, s_pad, -1), 1, 2),
                   tq, pl_len, s_valid)[:, :t]

    x1 = _merge_call(x2d, ya.reshape(n, -1), yb.reshape(n, -1), yc.reshape(n, -1), merge_c, tm)
    x2 = _ffn_call(x1, *ffn_c, tm)
    return x2.reshape(b, t, d), rows


def kernel(x_prompt, x_sample, cache_fox_k, cache_fox_v, cache_fox_logf, cache_dsa_k, cache_dsa_v, cache_idx_k, cache_mla_ckv, cache_mla_kpe, rel_bias, norm_mix, w_in, fox_gq, fox_gk, fox_bf, dsa_gq, dsa_gk, idx_gk, mla_gqa, mla_wqb, mla_gqn, mla_gqr, mla_gkv, mla_gkr, mla_wkvb, mla_gkn, w_fox_out, w_dsa_out, w_mla_out, w_gate, w_o, norm_ffn, w_ffn_in, w_ffn_out):
    caches = (cache_fox_k, cache_fox_v, cache_fox_logf, cache_dsa_k, cache_dsa_v,
              cache_idx_k, cache_mla_ckv, cache_mla_kpe)
    yp, ys = x_prompt, x_sample
    p_rows, s_rows, bias_cache = [], [], {}
    for i in range(norm_mix.shape[0]):
        p = dict(norm_mix=norm_mix[i], w_in=w_in[i], fox_gq=fox_gq[i], fox_gk=fox_gk[i],
                 fox_bf=fox_bf[i], dsa_gq=dsa_gq[i], dsa_gk=dsa_gk[i], idx_gk=idx_gk[i],
                 mla_gqa=mla_gqa[i], mla_wqb=mla_wqb[i], mla_gqn=mla_gqn[i], mla_gqr=mla_gqr[i],
                 mla_gkv=mla_gkv[i], mla_gkr=mla_gkr[i], mla_wkvb=mla_wkvb[i], mla_gkn=mla_gkn[i],
                 w_fox_out=w_fox_out[i], w_dsa_out=w_dsa_out[i], w_mla_out=w_mla_out[i],
                 w_gate=w_gate[i], w_o=w_o[i], norm_ffn=norm_ffn[i], w_ffn_in=w_ffn_in[i],
                 w_ffn_out=w_ffn_out[i])
        yp, rows_p = _layer(yp, None, p, rel_bias, bias_cache)
        ys, rows_s = _layer(ys, tuple(c[i] for c in caches), p, rel_bias, bias_cache)
        p_rows.append(rows_p)
        s_rows.append(rows_s)

    def st(rows, j):
        return jnp.stack([r[j] for r in rows], axis=0)

    return ((yp, ys) + tuple(st(p_rows, j) for j in range(8)) + tuple(st(s_rows, j) for j in range(8)))
```
